```python
import math
import jax, jax.numpy as jnp
from jax import lax
import numpy as np

D_MODEL = 1024
BATCH = 4
SEQ = 8192
DEPTH = 1
DEC_BATCH = 8
DEC_SEQ = 4096
PAST_LEN = 128

ATTN_WIDTH = D_MODEL // 2
SSM_WIDTH = D_MODEL - ATTN_WIDTH
N_HEADS = 4
HEAD_DIM = ATTN_WIDTH // (2 * N_HEADS)
SSM_GROUP = 16
N_SSM_GROUPS = SSM_WIDTH // SSM_GROUP
STATE = 64
D_FF = 2816
CONV_W = 3
Q_BLOCK = 128
NORM_EPS = 1e-6
SUBLN_EPS = 1e-5
STEP_MIN = 1e-3
STEP_MAX = 1e-1

kernel_name = "hybrid_diffattn_s5_encoder"


def rmsnorm(x, g, eps):
    xf = x.astype(jnp.float32)
    y = xf * lax.rsqrt(jnp.mean(xf * xf, axis=-1, keepdims=True) + eps)
    return (y * g.astype(jnp.float32)).astype(x.dtype)


def alibi_slopes():
    h = jnp.arange(1, N_HEADS + 1, dtype=jnp.float32)
    return jnp.exp2(-8.0 * h / N_HEADS)


def diff_attention(q, k, v, lam, slopes):
    B, S = q.shape[0], q.shape[1]
    nblk = S // Q_BLOCK
    scale = HEAD_DIM ** -0.5
    qb = q.reshape(B, nblk, Q_BLOCK, N_HEADS, 2, HEAD_DIM).transpose(1, 0, 2, 3, 4, 5)
    kpos = jnp.arange(S, dtype=jnp.int32)

    def block(args):
        q_blk, i = args
        qpos = i * Q_BLOCK + jnp.arange(Q_BLOCK, dtype=jnp.int32)
        dist = jnp.abs(qpos[:, None] - kpos[None, :]).astype(jnp.float32)
        bias = -slopes[:, None, None] * dist
        s = jnp.einsum("bqhcd,bkhcd->bhcqk", q_blk, k).astype(jnp.float32) * scale
        p = jax.nn.softmax(s + bias[None, :, None], axis=-1)
        w = p[:, :, 0] - lam * p[:, :, 1]
        return jnp.einsum("bhqk,bkhe->bqhe", w.astype(v.dtype), v)

    out = lax.map(block, (qb, jnp.arange(nblk, dtype=jnp.int32)))
    return out.transpose(1, 0, 2, 3, 4).reshape(B, S, N_HEADS, 2 * HEAD_DIM)


def _scan_combine(e1, e2):
    a1r, a1i, b1r, b1i = e1
    a2r, a2i, b2r, b2i = e2
    ar = a1r * a2r - a1i * a2i
    ai = a1r * a2i + a1i * a2r
    br = a2r * b1r - a2i * b1i + b2r
    bi = a2r * b1i + a2i * b1r + b2i
    return ar, ai, br, bi


def s5_direction(u, a_re, a_im, log_step, b_re, b_im, c_re, c_im, reverse):
    step = jnp.exp(log_step)[:, None]
    mag = jnp.exp(a_re * step)
    lb_re = mag * jnp.cos(a_im * step)
    lb_im = mag * jnp.sin(a_im * step)
    den = a_re * a_re + a_im * a_im
    nr = lb_re - 1.0
    f_re = (nr * a_re + lb_im * a_im) / den
    f_im = (lb_im * a_re - nr * a_im) / den
    bb_re = f_re[..., None] * b_re - f_im[..., None] * b_im
    bb_im = f_re[..., None] * b_im + f_im[..., None] * b_re
    bu_re = jnp.einsum("gpc,blgc->blgp", bb_re, u)
    bu_im = jnp.einsum("gpc,blgc->blgp", bb_im, u)
    lam_re = jnp.broadcast_to(lb_re, bu_re.shape)
    lam_im = jnp.broadcast_to(lb_im, bu_im.shape)
    _, _, h_re, h_im = lax.associative_scan(
        _scan_combine, (lam_re, lam_im, bu_re, bu_im), reverse=reverse, axis=1)
    return jnp.einsum("gcp,blgp->blgc", c_re, h_re) - jnp.einsum("gcp,blgp->blgc", c_im, h_im)


def s5_mixer(u, a_re, a_im, log_step, b_re, b_im, c_re, c_im, d, w_glu, b_glu, g_out):
    B, L, _ = u.shape
    uf = u.astype(jnp.float32).reshape(B, L, N_SSM_GROUPS, SSM_GROUP)
    f32 = lambda t: t.astype(jnp.float32)
    y_f = s5_direction(uf, f32(a_re[0]), f32(a_im[0]), f32(log_step[0]), f32(b_re[0]), f32(b_im[0]),
                       f32(c_re[0]), f32(c_im[0]), False)
    y_b = s5_direction(uf, f32(a_re[1]), f32(a_im[1]), f32(log_step[1]), f32(b_re[1]), f32(b_im[1]),
                       f32(c_re[1]), f32(c_im[1]), True)
    y = (y_f + y_b).reshape(B, L, SSM_WIDTH) + f32(d) * uf.reshape(B, L, SSM_WIDTH)
    y = jax.nn.gelu(y.astype(u.dtype))
    y = y * jax.nn.sigmoid(y @ w_glu + b_glu)
    return rmsnorm(y, g_out, NORM_EPS)


def dwconv_centred(h, w, b):
    hp = jnp.pad(h, ((0, 0), (1, 1), (0, 0)))
    return hp[:, :-2] * w[0] + hp[:, 1:-1] * w[1] + hp[:, 2:] * w[2] + b


def gated_conv_mlp(h, w_up, conv_w, conv_b, w_down):
    z = dwconv_centred(h @ w_up, conv_w, conv_b)
    gate, val = jnp.split(z, 2, axis=-1)
    return (jax.nn.gelu(gate) * val) @ w_down


def trunk(x, g_mix_norm, w_in, lambda_q1, lambda_k1, lambda_q2, lambda_k2, g_subln,
          ssm_a_re, ssm_a_im, ssm_log_step, ssm_b_re, ssm_b_im, ssm_c_re, ssm_c_im, ssm_d,
          w_glu, b_glu, g_ssm_out, w_out, g_ffn_norm, w_up, conv_w, conv_b, w_down, g_final):
    B, S, _ = x.shape
    slopes = alibi_slopes()
    for l in range(DEPTH):
        lam_init = 0.8 - 0.6 * math.exp(-0.3 * l)
        n = rmsnorm(x, g_mix_norm[l], NORM_EPS)
        proj = n @ w_in[l]
        q = proj[..., :ATTN_WIDTH].reshape(B, S, N_HEADS, 2, HEAD_DIM)
        k = proj[..., ATTN_WIDTH:2 * ATTN_WIDTH].reshape(B, S, N_HEADS, 2, HEAD_DIM)
        v = proj[..., 2 * ATTN_WIDTH:3 * ATTN_WIDTH].reshape(B, S, N_HEADS, 2 * HEAD_DIM)
        u = proj[..., 3 * ATTN_WIDTH:]
        lam = (jnp.exp(jnp.sum(lambda_q1[l].astype(jnp.float32) * lambda_k1[l].astype(jnp.float32)))
               - jnp.exp(jnp.sum(lambda_q2[l].astype(jnp.float32) * lambda_k2[l].astype(jnp.float32)))
               + lam_init)
        a = diff_attention(q, k, v, lam, slopes)
        a = (rmsnorm(a, g_subln[l], SUBLN_EPS) * (1.0 - lam_init)).reshape(B, S, ATTN_WIDTH)
        s = s5_mixer(u, ssm_a_re[l], ssm_a_im[l], ssm_log_step[l], ssm_b_re[l], ssm_b_im[l],
                     ssm_c_re[l], ssm_c_im[l], ssm_d[l], w_glu[l], b_glu[l], g_ssm_out[l])
        x = x + jnp.concatenate([a, s], axis=-1) @ w_out[l]
        h = rmsnorm(x, g_ffn_norm[l], NORM_EPS)
        x = x + gated_conv_mlp(h, w_up[l], conv_w[l], conv_b[l], w_down[l])
    return rmsnorm(x, g_final, NORM_EPS)


def setup_inputs(seed: int = 0) -> dict:
    key = jax.random.key(seed)
    ks = jax.random.split(key, 32)
    f = jnp.float32
    nrm = lambda k, shape, s: jax.random.normal(k, shape, f) * s
    G, P, C = N_SSM_GROUPS, STATE, SSM_GROUP
    a_im_init = jnp.pi * jnp.arange(P, dtype=f)
    return {
        "x_prompt": jax.random.normal(ks[0], (BATCH, SEQ, D_MODEL), f),
        "x_sample": jax.random.normal(ks[1], (DEC_BATCH, DEC_SEQ, D_MODEL), f),
        "g_mix_norm": 1.0 + nrm(ks[2], (DEPTH, D_MODEL), 0.02),
        "w_in": nrm(ks[3], (DEPTH, D_MODEL, 3 * ATTN_WIDTH + SSM_WIDTH), D_MODEL ** -0.5),
        "lambda_q1": nrm(ks[4], (DEPTH, HEAD_DIM), 0.1),
        "lambda_k1": nrm(ks[5], (DEPTH, HEAD_DIM), 0.1),
        "lambda_q2": nrm(ks[6], (DEPTH, HEAD_DIM), 0.1),
        "lambda_k2": nrm(ks[7], (DEPTH, HEAD_DIM), 0.1),
        "g_subln": 1.0 + nrm(ks[8], (DEPTH, 2 * HEAD_DIM), 0.02),
        "ssm_a_re": -0.5 + nrm(ks[9], (DEPTH, 2, G, P), 0.01),
        "ssm_a_im": a_im_init + nrm(ks[10], (DEPTH, 2, G, P), 0.01),
        "ssm_log_step": jax.random.uniform(ks[11], (DEPTH, 2, G), f,
                                           math.log(STEP_MIN), math.log(STEP_MAX)),
        "ssm_b_re": nrm(ks[12], (DEPTH, 2, G, P, C), (2.0 * C) ** -0.5),
        "ssm_b_im": nrm(ks[13], (DEPTH, 2, G, P, C), (2.0 * C) ** -0.5),
        "ssm_c_re": nrm(ks[14], (DEPTH, 2, G, C, P), (2.0 * P) ** -0.5),
        "ssm_c_im": nrm(ks[15], (DEPTH, 2, G, C, P), (2.0 * P) ** -0.5),
        "ssm_d": nrm(ks[16], (DEPTH, SSM_WIDTH), 1.0),
        "w_glu": nrm(ks[17], (DEPTH, SSM_WIDTH, SSM_WIDTH), SSM_WIDTH ** -0.5),
        "b_glu": nrm(ks[18], (DEPTH, SSM_WIDTH), 0.02),
        "g_ssm_out": 1.0 + nrm(ks[19], (DEPTH, SSM_WIDTH), 0.02),
        "w_out": nrm(ks[20], (DEPTH, D_MODEL, D_MODEL), D_MODEL ** -0.5),
        "g_ffn_norm": 1.0 + nrm(ks[21], (DEPTH, D_MODEL), 0.02),
        "w_up": nrm(ks[22], (DEPTH, D_MODEL, 2 * D_FF), D_MODEL ** -0.5),
        "conv_w": nrm(ks[23], (DEPTH, CONV_W, 2 * D_FF), CONV_W ** -0.5),
        "conv_b": nrm(ks[24], (DEPTH, 2 * D_FF), 0.02),
        "w_down": nrm(ks[25], (DEPTH, D_FF, D_MODEL), D_FF ** -0.5),
        "g_final": 1.0 + nrm(ks[26], (D_MODEL,), 0.02),
    }


def reference(x_prompt, x_sample, g_mix_norm, w_in, lambda_q1, lambda_k1, lambda_q2, lambda_k2,
              g_subln, ssm_a_re, ssm_a_im, ssm_log_step, ssm_b_re, ssm_b_im, ssm_c_re, ssm_c_im,
              ssm_d, w_glu, b_glu, g_ssm_out, w_out, g_ffn_norm, w_up, conv_w, conv_b, w_down,
              g_final):
    y_prompt = trunk(x_prompt, g_mix_norm, w_in, lambda_q1, lambda_k1, lambda_q2, lambda_k2, g_subln,
                     ssm_a_re, ssm_a_im, ssm_log_step, ssm_b_re, ssm_b_im, ssm_c_re, ssm_c_im, ssm_d,
                     w_glu, b_glu, g_ssm_out, w_out, g_ffn_norm, w_up, conv_w, conv_b, w_down, g_final)
    y_sample = trunk(x_sample, g_mix_norm, w_in, lambda_q1, lambda_k1, lambda_q2, lambda_k2, g_subln,
                     ssm_a_re, ssm_a_im, ssm_log_step, ssm_b_re, ssm_b_im, ssm_c_re, ssm_c_im, ssm_d,
                     w_glu, b_glu, g_ssm_out, w_out, g_ffn_norm, w_up, conv_w, conv_b, w_down, g_final)
    return (y_prompt, y_sample)
```

```python
import functools
import math

import jax
import jax.numpy as jnp
from jax import lax
from jax.experimental import pallas as pl
from jax.experimental.pallas import tpu as pltpu

F32 = jnp.float32
BF16 = jnp.bfloat16

D_MODEL = 1024
ATTN_WIDTH = 512
SSM_WIDTH = 512
N_HEADS = 4
HEAD_DIM = 64
HEAD_WIDTH = 2 * HEAD_DIM
SSM_GROUP = 16
N_SSM_GROUPS = 32
N_GROUP_PAIRS = N_SSM_GROUPS // 2
STATE = 64
PAIR_STATE = 2 * STATE
D_FF = 2816
NORM_EPS = 1e-6
SUBLN_EPS = 1e-5
LAM_INIT = 0.8 - 0.6 * math.exp(-0.3 * 0)

SSM_CHUNK = 32
CHUNK_W = SSM_CHUNK * SSM_GROUP
LAG_W = 2 * CHUNK_W
POW_BITS = SSM_CHUNK.bit_length()
N_SLABS = 4

V7X_SUBLANES = 8
V7X_BF16_ROWS = 16
V7X_VMEM_LIMIT = 56 * 1024 * 1024

HI = lax.Precision.HIGHEST


def _params(sem, vmem=V7X_VMEM_LIMIT):
    return pltpu.CompilerParams(dimension_semantics=sem, vmem_limit_bytes=vmem)


def _rms(x, g, eps):
    return x * lax.rsqrt(jnp.mean(x * x, axis=-1, keepdims=True) + eps) * g


def _inproj_body(x_ref, g_ref, w_ref, q_ref, k_ref, v_ref, u_ref):
    n = _rms(x_ref[...], g_ref[...], NORM_EPS).astype(BF16)
    proj = jnp.dot(n, w_ref[...], preferred_element_type=F32)
    a = ATTN_WIDTH
    q_ref[...] = (proj[:, :a] * (HEAD_DIM ** -0.5)).astype(BF16)
    k_ref[...] = proj[:, a:2 * a].astype(BF16)
    v_ref[...] = proj[:, 2 * a:3 * a].astype(BF16)
    u_ref[...] = proj[:, 3 * a:]


def _inproj(x2, g, w_bf, tm):
    n_tok = x2.shape[0]
    row = lambda w: pl.BlockSpec((tm, w), lambda i: (i, 0))
    return pl.pallas_call(
        _inproj_body,
        grid=(n_tok // tm,),
        in_specs=[row(D_MODEL),
                  pl.BlockSpec((1, D_MODEL), lambda i: (0, 0)),
                  pl.BlockSpec(w_bf.shape, lambda i: (0, 0))],
        out_specs=[row(ATTN_WIDTH), row(ATTN_WIDTH), row(ATTN_WIDTH), row(SSM_WIDTH)],
        out_shape=[jax.ShapeDtypeStruct((n_tok, ATTN_WIDTH), BF16)] * 3
        + [jax.ShapeDtypeStruct((n_tok, SSM_WIDTH), F32)],
        compiler_params=_params(("parallel",)),
        name="inproj",
    )(x2, g, w_bf)


def _attn_body(q_ref, k_ref, v_ref, lq1_ref, lk1_ref, lq2_ref, lk2_ref, g_ref, o_ref,
               qs_ref, d0_ref, m_ref, l_ref, acc_ref, *, tq, tk, nk):
    h = pl.program_id(1)
    i = pl.program_id(2)
    slope = jnp.float32(0.0)
    for head in range(N_HEADS):
        slope = jnp.where(h == head, jnp.float32(2.0 ** (-8.0 * (head + 1) / N_HEADS)), slope)

    q = q_ref[0]
    lane = lax.broadcasted_iota(jnp.int32, q.shape, 1)
    zero = jnp.zeros_like(q)
    qs_ref[:tq, :] = jnp.where(lane < HEAD_DIM, q, zero)
    qs_ref[tq:, :] = jnp.where(lane < HEAD_DIM, zero, q)

    r = lax.broadcasted_iota(jnp.int32, (tq, tk), 0)
    c = lax.broadcasted_iota(jnp.int32, (tq, tk), 1)
    d0_ref[...] = (r - c).astype(F32)

    m_ref[...] = jnp.full(m_ref.shape, -jnp.inf, F32)
    l_ref[...] = jnp.zeros(l_ref.shape, F32)
    acc_ref[...] = jnp.zeros(acc_ref.shape, F32)

    def kv_step(j, carry):
        start = pl.multiple_of(j * tk, tk)
        ks = k_ref[0, pl.ds(start, tk), :]
        vs = v_ref[0, pl.ds(start, tk), :]
        s = lax.dot_general(qs_ref[...], ks, (((1,), (1,)), ((), ())),
                            preferred_element_type=F32)
        off = (i * tq - j * tk).astype(F32)
        bias = -slope * jnp.abs(d0_ref[...] + off)
        s = s + jnp.concatenate([bias, bias], axis=0)
        m_prev = m_ref[...]
        m_new = jnp.maximum(m_prev, jnp.max(s, axis=-1, keepdims=True))
        alpha = jnp.exp(m_prev - m_new)
        p = jnp.exp(s - m_new)
        l_ref[...] = alpha * l_ref[...] + jnp.sum(p, axis=-1, keepdims=True)
        acc_ref[...] = alpha * acc_ref[...] + jnp.dot(p.astype(BF16), vs,
                                                      preferred_element_type=F32)
        m_ref[...] = m_new
        return carry

    lax.fori_loop(0, nk, kv_step, 0)

    lam = (jnp.exp(jnp.sum(lq1_ref[...] * lk1_ref[...], axis=-1, keepdims=True))
           - jnp.exp(jnp.sum(lq2_ref[...] * lk2_ref[...], axis=-1, keepdims=True))
           + LAM_INIT)
    o = acc_ref[...] / l_ref[...]
    a = o[:tq] - lam * o[tq:]
    o_ref[0] = (_rms(a, g_ref[...], SUBLN_EPS) * (1.0 - LAM_INIT)).astype(o_ref.dtype)


def _attention(q, k, v, lq1, lk1, lq2, lk2, g_subln, tq, tk):
    b, s, _ = q.shape
    nk = s // tk
    vec = lambda n: pl.BlockSpec((1, n), lambda bi, hi, qi: (0, 0))
    kv_spec = pl.BlockSpec((1, s, HEAD_WIDTH), lambda bi, hi, qi: (bi, 0, hi))
    return pl.pallas_call(
        functools.partial(_attn_body, tq=tq, tk=tk, nk=nk),
        grid=(b, N_HEADS, s // tq),
        in_specs=[pl.BlockSpec((1, tq, HEAD_WIDTH), lambda bi, hi, qi: (bi, qi, hi)),
                  kv_spec, kv_spec,
                  vec(HEAD_DIM), vec(HEAD_DIM), vec(HEAD_DIM), vec(HEAD_DIM),
                  vec(HEAD_WIDTH)],
        out_specs=pl.BlockSpec((1, tq, HEAD_WIDTH), lambda bi, hi, qi: (bi, qi, hi)),
        out_shape=jax.ShapeDtypeStruct((b, s, ATTN_WIDTH), BF16),
        scratch_shapes=[pltpu.VMEM((2 * tq, HEAD_WIDTH), BF16),
                        pltpu.VMEM((tq, tk), F32),
                        pltpu.VMEM((2 * tq, 1), F32),
                        pltpu.VMEM((2 * tq, 1), F32),
                        pltpu.VMEM((2 * tq, HEAD_WIDTH), F32)],
        compiler_params=_params(("parallel", "parallel", "parallel")),
        name="diff_attention",
    )(q, k, v, lq1, lk1, lq2, lk2, g_subln)


def _cpow(lre, lim, n):
    shape = jnp.broadcast_shapes(lre.shape, n.shape)
    pre = jnp.ones(shape, F32)
    pim = jnp.zeros(shape, F32)
    bre, bim = lre, lim
    for bit in range(POW_BITS):
        on = ((n >> bit) & 1) == 1
        mre = jnp.where(on, bre, 1.0)
        mim = jnp.where(on, bim, 0.0)
        pre, pim = pre * mre - pim * mim, pre * mim + pim * mre
        bre, bim = bre * bre - bim * bim, 2.0 * bre * bim
    return pre, pim


def _discretise(a_re, a_im, log_step):
    step = jnp.exp(log_step)
    mag = jnp.exp(a_re * step)
    lre = mag * jnp.cos(a_im * step)
    lim = mag * jnp.sin(a_im * step)
    den = a_re * a_re + a_im * a_im
    nr = lre - 1.0
    fre = (nr * a_re + lim * a_im) / den
    fim = (lim * a_re - nr * a_im) / den
    return lre, lim, fre, fim


def _ssm_weights_body(arow_re, arow_im, acol_re, acol_im, lsrow, lscol, bt_re, bt_im,
                      ct_re, ct_im, wt_ref, ws_ref, wo_ref, at_ref):
    t, cw = SSM_CHUNK, CHUNK_W
    lane_pair = lax.broadcasted_iota(jnp.int32, (1, PAIR_STATE), 1) // STATE
    row_pair = lax.broadcasted_iota(jnp.int32, (PAIR_STATE, 1), 0) // STATE

    ch = lax.broadcasted_iota(jnp.int32, (SSM_GROUP, LAG_W), 0)
    ln = lax.broadcasted_iota(jnp.int32, (SSM_GROUP, LAG_W), 1)
    tile_lag = jnp.where((ln % SSM_GROUP) == ch, 1.0, 0.0).astype(F32)

    step_rows = lax.broadcasted_iota(jnp.int32, (cw, 1), 0) // SSM_GROUP
    slot = lax.broadcasted_iota(jnp.int32, (1, LAG_W), 1) // SSM_GROUP
    step_lanes = lax.broadcasted_iota(jnp.int32, (1, cw), 1) // SSM_GROUP

    zt = [jnp.zeros((SSM_GROUP, LAG_W), F32) for _ in range(2)]
    for d in range(2):
        lre, lim, fre, fim = _discretise(arow_re[d, 0], arow_im[d, 0], lsrow[d, 0])
        bre, bim = bt_re[d, 0], bt_im[d, 0]
        bbre = fre * bre - fim * bim
        bbim = fre * bim + fim * bre
        expo = (t - 1 - step_rows) if d == 0 else step_rows
        pre, pim = _cpow(lre, lim, expo)
        tbre = jnp.tile(bbre, (t, 1))
        tbim = jnp.tile(bbim, (t, 1))
        inj_re = pre * tbre - pim * tbim
        inj_im = pre * tbim + pim * tbre
        for g2 in range(2):
            keep = lane_pair == g2
            rows = pl.ds(g2 * cw, cw)
            ws_ref[0, rows, pl.ds((2 * d) * PAIR_STATE, PAIR_STATE)] = (
                jnp.where(keep, inj_re, 0.0).astype(ws_ref.dtype))
            ws_ref[0, rows, pl.ds((2 * d + 1) * PAIR_STATE, PAIR_STATE)] = (
                jnp.where(keep, inj_im, 0.0).astype(ws_ref.dtype))
        dre, dim_ = _cpow(lre, lim, jnp.full((1, 1), t, jnp.int32))
        at_ref[0, pl.ds(2 * d, 1), :] = dre
        at_ref[0, pl.ds(2 * d + 1, 1), :] = dim_

        cre, cim, _, _ = _discretise(acol_re[d, 0], acol_im[d, 0], lscol[d, 0])
        ctl_re = jnp.dot(ct_re[d, 0], tile_lag, precision=HI, preferred_element_type=F32)
        ctl_im = jnp.dot(ct_im[d, 0], tile_lag, precision=HI, preferred_element_type=F32)
        lag = (slot - (t - 1)) if d == 0 else ((t - 1) - slot)
        valid = (lag >= 0) & (slot < 2 * t - 1)
        qre, qim = _cpow(cre, cim, jnp.maximum(lag, 0))
        r_re = jnp.where(valid, qre * ctl_re - qim * ctl_im, 0.0)
        r_im = jnp.where(valid, qre * ctl_im + qim * ctl_re, 0.0)
        for g2 in range(2):
            keep = lane_pair == g2
            zt[g2] = (zt[g2]
                      + jnp.dot(jnp.where(keep, bbre, 0.0), r_re, precision=HI,
                                preferred_element_type=F32)
                      - jnp.dot(jnp.where(keep, bbim, 0.0), r_im, precision=HI,
                                preferred_element_type=F32))

        out_pow = (step_lanes + 1) if d == 0 else (t - step_lanes)
        ore, oim = _cpow(cre, cim, out_pow)
        c_re = ctl_re[:, :cw]
        c_im = ctl_im[:, :cw]
        rd_re = ore * c_re - oim * c_im
        rd_im = -(ore * c_im + oim * c_re)
        for g2 in range(2):
            keep = row_pair == g2
            cols = pl.ds(g2 * cw, cw)
            wo_ref[0, pl.ds((2 * d) * PAIR_STATE, PAIR_STATE), cols] = (
                jnp.where(keep, rd_re, 0.0).astype(wo_ref.dtype))
            wo_ref[0, pl.ds((2 * d + 1) * PAIR_STATE, PAIR_STATE), cols] = (
                jnp.where(keep, rd_im, 0.0).astype(wo_ref.dtype))

    for g2 in range(2):
        for s in range(t):
            off = (t - 1 - s) * SSM_GROUP
            shifted = zt[g2] if off == 0 else pltpu.roll(zt[g2], LAG_W - off, 1)
            wt_ref[0, g2, pl.ds(s * SSM_GROUP, SSM_GROUP), :] = (
                shifted[:, :cw].astype(wt_ref.dtype))


def _ssm_weights(a_re, a_im, log_step, b_re, b_im, c_re, c_im):
    np_, g2p = N_GROUP_PAIRS, PAIR_STATE
    arow = lambda a: a.reshape(2, np_, 1, g2p)
    acol = lambda a: a.reshape(2, np_, g2p, 1)
    ls = jnp.repeat(log_step, STATE, axis=-1)
    bt = lambda w: (w.reshape(2, np_, 2, STATE, SSM_GROUP)
                    .transpose(0, 1, 4, 2, 3).reshape(2, np_, SSM_GROUP, g2p))
    ct = lambda w: (w.reshape(2, np_, 2, SSM_GROUP, STATE)
                    .transpose(0, 1, 2, 4, 3).reshape(2, np_, g2p, SSM_GROUP))
    ins = [arow(a_re), arow(a_im), acol(a_re), acol(a_im), arow(ls), acol(ls),
           bt(b_re), bt(b_im), ct(c_re), ct(c_im)]
    spec = lambda x: pl.BlockSpec((2, 1) + x.shape[2:], lambda k: (0, k, 0, 0))
    return pl.pallas_call(
        _ssm_weights_body,
        grid=(np_,),
        in_specs=[spec(x) for x in ins],
        out_specs=[pl.BlockSpec((1, 2, CHUNK_W, CHUNK_W), lambda k: (k, 0, 0, 0)),
                   pl.BlockSpec((1, 2 * CHUNK_W, N_SLABS * g2p), lambda k: (k, 0, 0)),
                   pl.BlockSpec((1, N_SLABS * g2p, 2 * CHUNK_W), lambda k: (k, 0, 0)),
                   pl.BlockSpec((1, N_SLABS, g2p), lambda k: (k, 0, 0))],
        out_shape=[jax.ShapeDtypeStruct((np_, 2, CHUNK_W, CHUNK_W), BF16),
                   jax.ShapeDtypeStruct((np_, 2 * CHUNK_W, N_SLABS * g2p), BF16),
                   jax.ShapeDtypeStruct((np_, N_SLABS * g2p, 2 * CHUNK_W), BF16),
                   jax.ShapeDtypeStruct((np_, N_SLABS, g2p), F32)],
        compiler_params=_params(("parallel",)),
        name="ssm_weights",
    )(*ins)


def _ssm_inject_body(x_ref, ws_ref, s_ref):
    s = jnp.dot(x_ref[0], ws_ref[0], preferred_element_type=F32)
    for slab in range(N_SLABS):
        s_ref[slab] = s[:, slab * PAIR_STATE:(slab + 1) * PAIR_STATE]


def _ssm_inject(x, ws):
    np_, r, _ = x.shape
    return pl.pallas_call(
        _ssm_inject_body,
        grid=(np_,),
        in_specs=[pl.BlockSpec((1, r, 2 * CHUNK_W), lambda k: (k, 0, 0)),
                  pl.BlockSpec((1,) + ws.shape[1:], lambda k: (k, 0, 0))],
        out_specs=pl.BlockSpec((N_SLABS, r, PAIR_STATE), lambda k: (0, 0, k)),
        out_shape=jax.ShapeDtypeStruct((N_SLABS, r, np_ * PAIR_STATE), F32),
        compiler_params=_params(("parallel",)),
        name="ssm_inject",
    )(x, ws)


def _ssm_scan_body(s_ref, at_ref, hin_ref, *, batch, n_tiles):
    sub = V7X_SUBLANES // batch
    lanes = s_ref.shape[-1]
    a = [at_ref[pl.ds(slab, 1), :] for slab in range(N_SLABS)]

    def sweep(d, tile, carry):
        hre, him = carry
        are, aim = a[2 * d], a[2 * d + 1]
        row0 = pl.multiple_of(tile * V7X_SUBLANES, V7X_SUBLANES)
        sre = s_ref[2 * d, pl.ds(row0, V7X_SUBLANES), :]
        sim = s_ref[2 * d + 1, pl.ds(row0, V7X_SUBLANES), :]
        out_re = [None] * sub
        out_im = [None] * sub
        order = range(sub) if d == 0 else range(sub - 1, -1, -1)
        for q in order:
            out_re[q], out_im[q] = hre, him
            xr = sre[q * batch:(q + 1) * batch]
            xi = sim[q * batch:(q + 1) * batch]
            hre, him = are * hre - aim * him + xr, are * him + aim * hre + xi
        hin_ref[2 * d, pl.ds(row0, V7X_SUBLANES), :] = jnp.concatenate(out_re, axis=0)
        hin_ref[2 * d + 1, pl.ds(row0, V7X_SUBLANES), :] = jnp.concatenate(out_im, axis=0)
        return hre, him

    zero = (jnp.zeros((batch, lanes), F32), jnp.zeros((batch, lanes), F32))
    lax.fori_loop(0, n_tiles, lambda i, c: sweep(0, i, c), zero)
    lax.fori_loop(0, n_tiles, lambda i, c: sweep(1, n_tiles - 1 - i, c), zero)


def _ssm_scan(s, at, batch, lane_block):
    _, r, lanes = s.shape
    assert V7X_SUBLANES % batch == 0 and r % V7X_SUBLANES == 0
    return pl.pallas_call(
        functools.partial(_ssm_scan_body, batch=batch, n_tiles=r // V7X_SUBLANES),
        grid=(lanes // lane_block,),
        in_specs=[pl.BlockSpec((N_SLABS, r, lane_block), lambda k: (0, 0, k)),
                  pl.BlockSpec((N_SLABS, lane_block), lambda k: (0, k))],
        out_specs=pl.BlockSpec((N_SLABS, r, lane_block), lambda k: (0, 0, k)),
        out_shape=jax.ShapeDtypeStruct(s.shape, F32),
        compiler_params=_params(("parallel",)),
        name="ssm_scan",
    )(s, at)


def _ssm_mix_body(x_ref, wt_ref, hin_ref, wo_ref, y_ref):
    x = x_ref[0]
    hin = jnp.concatenate([hin_ref[slab] for slab in range(N_SLABS)], axis=1).astype(BF16)
    carried = jnp.dot(hin, wo_ref[0], preferred_element_type=F32)
    for g2 in range(2):
        cols = slice(g2 * CHUNK_W, (g2 + 1) * CHUNK_W)
        y_ref[0, :, cols] = carried[:, cols] + jnp.dot(x[:, cols], wt_ref[0, g2],
                                                       preferred_element_type=F32)


def _ssm_mix(x, wt, hin, wo):
    np_, r, _ = x.shape
    return pl.pallas_call(
        _ssm_mix_body,
        grid=(np_,),
        in_specs=[pl.BlockSpec((1, r, 2 * CHUNK_W), lambda k: (k, 0, 0)),
                  pl.BlockSpec((1,) + wt.shape[1:], lambda k: (k, 0, 0, 0)),
                  pl.BlockSpec((N_SLABS, r, PAIR_STATE), lambda k: (0, 0, k)),
                  pl.BlockSpec((1,) + wo.shape[1:], lambda k: (k, 0, 0))],
        out_specs=pl.BlockSpec((1, r, 2 * CHUNK_W), lambda k: (k, 0, 0)),
        out_shape=jax.ShapeDtypeStruct((np_, r, 2 * CHUNK_W), F32),
        compiler_params=_params(("parallel",)),
        name="ssm_mix",
    )(x, wt, hin, wo)


def _outproj_body(x_ref, a_ref, y_ref, u_ref, d_ref, wg_ref, bg_ref, gs_ref, wo_ref, gf_ref,
                  x1_ref, h_ref):
    y = jax.nn.gelu(y_ref[...] + d_ref[...] * u_ref[...])
    gate = jnp.dot(y.astype(BF16), wg_ref[...], preferred_element_type=F32) + bg_ref[...]
    s = _rms(y * jax.nn.sigmoid(gate), gs_ref[...], NORM_EPS)
    mixed = (jnp.dot(a_ref[...], wo_ref[:ATTN_WIDTH, :], preferred_element_type=F32)
             + jnp.dot(s.astype(BF16), wo_ref[ATTN_WIDTH:, :], preferred_element_type=F32))
    x1 = x_ref[...] + mixed
    x1_ref[...] = x1
    h_ref[...] = _rms(x1, gf_ref[...], NORM_EPS).astype(h_ref.dtype)


def _outproj(x2, a2, y2, u2, d, w_glu, b_glu, g_ssm, w_out, g_ffn, tm):
    n_tok = x2.shape[0]
    row = lambda w: pl.BlockSpec((tm, w), lambda i: (i, 0))
    full = lambda x: pl.BlockSpec(x.shape, lambda i: (0, 0))
    return pl.pallas_call(
        _outproj_body,
        grid=(n_tok // tm,),
        in_specs=[row(D_MODEL), row(ATTN_WIDTH), row(SSM_WIDTH), row(SSM_WIDTH),
                  full(d), full(w_glu), full(b_glu), full(g_ssm), full(w_out), full(g_ffn)],
        out_specs=[row(D_MODEL), row(D_MODEL)],
        out_shape=[jax.ShapeDtypeStruct((n_tok, D_MODEL), F32),
                   jax.ShapeDtypeStruct((n_tok, D_MODEL), BF16)],
        compiler_params=_params(("parallel",)),
        name="outproj",
    )(x2, a2, y2, u2, d, w_glu, b_glu, g_ssm, w_out, g_ffn)


def _ffn_body(h_ref, hp_ref, hn_ref, x1_ref, wu_ref, cw_ref, cb_ref, wd_ref, gf_ref, o_ref,
              hcat_ref, *, tm, tiles_per_seq, ff_chunk):
    i = pl.program_id(0)
    halo = V7X_BF16_ROWS
    first = (i % tiles_per_seq) == 0
    last = (i % tiles_per_seq) == tiles_per_seq - 1
    zeros = jnp.zeros((halo, D_MODEL), hcat_ref.dtype)
    hcat_ref[:halo, :] = jnp.where(first, zeros, hp_ref[...])
    hcat_ref[halo:halo + tm, :] = h_ref[...]
    hcat_ref[halo + tm:, :] = jnp.where(last, zeros, hn_ref[...])
    hcat = hcat_ref[...]
    rows = tm + 2 * halo

    def conv(z, col0):
        cols = pl.ds(col0, ff_chunk)
        prev = pltpu.roll(z, 1, 0)[halo:halo + tm]
        nxt = pltpu.roll(z, rows - 1, 0)[halo:halo + tm]
        return (prev * cw_ref[pl.ds(0, 1), cols] + z[halo:halo + tm] * cw_ref[pl.ds(1, 1), cols]
                + nxt * cw_ref[pl.ds(2, 1), cols] + cb_ref[:, cols])

    acc = jnp.zeros((tm, D_MODEL), F32)
    for c in range(D_FF // ff_chunk):
        g0 = c * ff_chunk
        v0 = D_FF + c * ff_chunk
        zg = jnp.dot(hcat, wu_ref[:, pl.ds(g0, ff_chunk)], preferred_element_type=F32)
        zv = jnp.dot(hcat, wu_ref[:, pl.ds(v0, ff_chunk)], preferred_element_type=F32)
        act = (jax.nn.gelu(conv(zg, g0)) * conv(zv, v0)).astype(BF16)
        acc = acc + jnp.dot(act, wd_ref[pl.ds(g0, ff_chunk), :], preferred_element_type=F32)
    o_ref[...] = _rms(x1_ref[...] + acc, gf_ref[...], NORM_EPS)


def _ffn(h2, x1, w_up, conv_w, conv_b, w_down, g_final, tm, seq, ff_chunk):
    n_tok = h2.shape[0]
    halo = V7X_BF16_ROWS
    per = tm // halo
    n_halo = n_tok // halo
    row = lambda w: pl.BlockSpec((tm, w), lambda i: (i, 0))
    once = lambda x: pl.BlockSpec(x.shape, lambda i: (0, 0), pipeline_mode=pl.Buffered(1))
    return pl.pallas_call(
        functools.partial(_ffn_body, tm=tm, tiles_per_seq=seq // tm, ff_chunk=ff_chunk),
        grid=(n_tok // tm,),
        in_specs=[row(D_MODEL),
                  pl.BlockSpec((halo, D_MODEL), lambda i: (jnp.maximum(i * per - 1, 0), 0)),
                  pl.BlockSpec((halo, D_MODEL),
                               lambda i: (jnp.minimum((i + 1) * per, n_halo - 1), 0)),
                  row(D_MODEL), once(w_up), once(conv_w), once(conv_b), once(w_down),
                  once(g_final)],
        out_specs=row(D_MODEL),
        out_shape=jax.ShapeDtypeStruct((n_tok, D_MODEL), F32),
        scratch_shapes=[pltpu.VMEM((tm + 2 * halo, D_MODEL), BF16)],
        compiler_params=_params(("parallel",)),
        name="ffn",
    )(h2, h2, h2, x1, w_up, conv_w, conv_b, w_down, g_final)


def _tile(pref, n):
    t = min(pref, n)
    assert n % t == 0, (pref, n)
    return t


def _to_chunks(u, batch, seq):
    nc = seq // SSM_CHUNK
    x = u.astype(BF16).reshape(batch, nc, SSM_CHUNK, N_GROUP_PAIRS, 2, SSM_GROUP)
    return x.transpose(3, 1, 0, 4, 2, 5).reshape(N_GROUP_PAIRS, nc * batch, 2 * CHUNK_W)


def _from_chunks(y, batch, seq):
    nc = seq // SSM_CHUNK
    y = y.reshape(N_GROUP_PAIRS, nc, batch, 2, SSM_CHUNK, SSM_GROUP)
    return y.transpose(2, 1, 4, 0, 3, 5).reshape(batch * seq, SSM_WIDTH)


def _trunk(x, p, ssm_w):
    batch, seq, _ = x.shape
    n_tok = batch * seq
    x2 = x.reshape(n_tok, D_MODEL)
    tm = _tile(512, seq)

    q, k, v, u = _inproj(x2, p["g_mix"], p["w_in"], tm)

    shape3 = (batch, seq, ATTN_WIDTH)
    a = _attention(q.reshape(shape3), k.reshape(shape3), v.reshape(shape3),
                   p["lq1"], p["lk1"], p["lq2"], p["lk2"], p["g_subln"],
                   _tile(256, seq), _tile(512, seq))

    wt, ws, wo, at = ssm_w
    xc = _to_chunks(u, batch, seq)
    s_in = _ssm_inject(xc, ws)
    at_flat = at.transpose(1, 0, 2).reshape(N_SLABS, N_GROUP_PAIRS * PAIR_STATE)
    hin = _ssm_scan(s_in, at_flat, batch, 4 * PAIR_STATE)
    y = _from_chunks(_ssm_mix(xc, wt, hin, wo), batch, seq)

    x1, h = _outproj(x2, a.reshape(n_tok, ATTN_WIDTH), y, u, p["d"], p["w_glu"], p["b_glu"],
                     p["g_ssm"], p["w_out"], p["g_ffn"], tm)
    out = _ffn(h, x1, p["w_up"], p["conv_w"], p["conv_b"], p["w_down"], p["g_final"],
               tm, seq, 256)
    return out.reshape(batch, seq, D_MODEL)


def kernel(x_prompt, x_sample, g_mix_norm, w_in, lambda_q1, lambda_k1, lambda_q2, lambda_k2, g_subln, ssm_a_re, ssm_a_im, ssm_log_step, ssm_b_re, ssm_b_im, ssm_c_re, ssm_c_im, ssm_d, w_glu, b_glu, g_ssm_out, w_out, g_ffn_norm, w_up, conv_w, conv_b, w_down, g_final):
    layer = 0
    vec = lambda t: t.reshape(1, -1).astype(F32)
    p = {
        "g_mix": vec(g_mix_norm[layer]), "w_in": w_in[layer].astype(BF16),
        "lq1": vec(lambda_q1[layer]), "lk1": vec(lambda_k1[layer]),
        "lq2": vec(lambda_q2[layer]), "lk2": vec(lambda_k2[layer]),
        "g_subln": vec(g_subln[layer]), "d": vec(ssm_d[layer]),
        "w_glu": w_glu[layer].astype(BF16), "b_glu": vec(b_glu[layer]),
        "g_ssm": vec(g_ssm_out[layer]), "w_out": w_out[layer].astype(BF16),
        "g_ffn": vec(g_ffn_norm[layer]), "w_up": w_up[layer].astype(BF16),
        "conv_w": conv_w[layer].astype(F32), "conv_b": vec(conv_b[layer]),
        "w_down": w_down[layer].astype(BF16), "g_final": vec(g_final),
    }
    f32 = lambda t: t[layer].astype(F32)
    ssm_w = _ssm_weights(f32(ssm_a_re), f32(ssm_a_im), f32(ssm_log_step), f32(ssm_b_re),
                         f32(ssm_b_im), f32(ssm_c_re), f32(ssm_c_im))
    return _trunk(x_prompt, p, ssm_w), _trunk(x_sample, p, ssm_w)
```

```python
import functools
import math

import jax
import jax.numpy as jnp
from jax import lax
from jax.experimental import pallas as pl
from jax.experimental.pallas import tpu as pltpu

F32 = jnp.float32
BF16 = jnp.bfloat16

D_MODEL = 1024
ATTN_WIDTH = 512
SSM_WIDTH = 512
N_HEADS = 4
HEAD_DIM = 64
HEAD_WIDTH = 2 * HEAD_DIM
SSM_GROUP = 16
N_SSM_GROUPS = 32
N_GROUP_PAIRS = N_SSM_GROUPS // 2
STATE = 64
PAIR_STATE = 2 * STATE
D_FF = 2816
NORM_EPS = 1e-6
SUBLN_EPS = 1e-5
LAM_INIT = 0.8 - 0.6 * math.exp(-0.3 * 0)
LOG2E = math.log2(math.e)
N_BIAS_LANES = 3

SSM_CHUNK = 32
CHUNK_W = SSM_CHUNK * SSM_GROUP
LAG_W = 2 * CHUNK_W
POW_BITS = SSM_CHUNK.bit_length()
N_SLABS = 4

V7X_SUBLANES = 8
V7X_BF16_ROWS = 16
V7X_VMEM_LIMIT = 56 * 1024 * 1024

HI = lax.Precision.HIGHEST


def _params(sem, vmem=V7X_VMEM_LIMIT):
    return pltpu.CompilerParams(dimension_semantics=sem, vmem_limit_bytes=vmem)


def _rms(x, g, eps):
    return x * lax.rsqrt(jnp.mean(x * x, axis=-1, keepdims=True) + eps) * g


def _inproj_body(x_ref, g_ref, wqku_ref, wvt_ref, q_ref, k_ref, vt_ref, u_ref):
    n = _rms(x_ref[...], g_ref[...], NORM_EPS).astype(BF16)
    proj = jnp.dot(n, wqku_ref[...], preferred_element_type=F32)
    a = ATTN_WIDTH
    q_ref[...] = (proj[:, :a] * (LOG2E * HEAD_DIM ** -0.5)).astype(BF16)
    k_ref[...] = proj[:, a:2 * a].astype(BF16)
    u_ref[...] = proj[:, 2 * a:]
    vt_ref[...] = lax.dot_general(wvt_ref[...], n, (((1,), (1,)), ((), ())),
                                  preferred_element_type=F32).astype(BF16)


def _inproj(x2, g, w_qku, w_vt, tm):
    n_tok = x2.shape[0]
    row = lambda w: pl.BlockSpec((tm, w), lambda i: (i, 0))
    full = lambda x: pl.BlockSpec(x.shape, lambda i: (0, 0))
    return pl.pallas_call(
        _inproj_body,
        grid=(n_tok // tm,),
        in_specs=[row(D_MODEL), full(g), full(w_qku), full(w_vt)],
        out_specs=[row(ATTN_WIDTH), row(ATTN_WIDTH),
                   pl.BlockSpec((ATTN_WIDTH, tm), lambda i: (0, i)), row(SSM_WIDTH)],
        out_shape=[jax.ShapeDtypeStruct((n_tok, ATTN_WIDTH), BF16),
                   jax.ShapeDtypeStruct((n_tok, ATTN_WIDTH), BF16),
                   jax.ShapeDtypeStruct((ATTN_WIDTH, n_tok), BF16),
                   jax.ShapeDtypeStruct((n_tok, SSM_WIDTH), F32)],
        compiler_params=_params(("parallel",)),
        name="inproj",
    )(x2, g, w_qku, w_vt)


def _attn_body(q_ref, k_ref, vt_ref, lq1_ref, lk1_ref, lq2_ref, lk2_ref, g_ref, o_ref,
               qa_ref, kaug_ref, s0_ref, s1_ref, mx_ref, beta_ref, m_ref, acc_ref,
               *, tq, tk, nk):
    h = pl.program_id(1)
    i = pl.program_id(2)
    tq2 = 2 * tq
    sigma = jnp.float32(0.0)
    for head in range(N_HEADS):
        sigma = jnp.where(h == head,
                          jnp.float32(LOG2E * 2.0 ** (-8.0 * (head + 1) / N_HEADS)), sigma)
    q0 = i * tq
    jd = q0 // tk

    q = q_ref[0]
    lane = lax.broadcasted_iota(jnp.int32, q.shape, 1)
    zero = jnp.zeros_like(q)
    ones3 = jnp.where(lane < N_BIAS_LANES, 1.0, 0.0).astype(BF16)
    qa_ref[:tq, :HEAD_WIDTH] = jnp.where(lane < HEAD_DIM, q, zero)
    qa_ref[tq:, :HEAD_WIDTH] = jnp.where(lane < HEAD_DIM, zero, q)
    qa_ref[:tq, HEAD_WIDTH:] = ones3
    qa_ref[tq:, HEAD_WIDTH:] = ones3

    koff = lax.broadcasted_iota(jnp.int32, (tk, HEAD_WIDTH), 0).astype(F32) * sigma
    klane = lax.broadcasted_iota(jnp.int32, (tk, HEAD_WIDTH), 1)
    hi = koff.astype(BF16).astype(F32)
    mid = (koff - hi).astype(BF16).astype(F32)
    lo = koff - hi - mid
    ktile = jnp.where(klane == 0, hi, jnp.where(klane == 1, mid, jnp.where(klane == 2, lo, 0.0)))
    kaug_ref[0] = ktile.astype(BF16)
    kaug_ref[1] = (-ktile).astype(BF16)

    ql = lax.broadcasted_iota(jnp.int32, (1, tq2), 1)
    qpos = (q0 + jnp.where(ql < tq, ql, ql - tq)).astype(F32)
    ones_rows = jnp.where(
        lax.broadcasted_iota(jnp.int32, (V7X_BF16_ROWS, tk), 0) == 0, 1.0, 0.0).astype(BF16)

    def block_of(n):
        return jnp.where(n == 0, jd, jnp.where(n - 1 < jd, n - 1, n))

    def scores(j):
        right = (j > jd).astype(jnp.int32)
        start = pl.multiple_of(j * tk, tk)
        ka = jnp.concatenate([k_ref[0, pl.ds(start, tk), :], kaug_ref[right]], axis=1)
        st = lax.dot_general(ka, qa_ref[...], (((1,), (1,)), ((), ())),
                             preferred_element_type=F32)
        sgn = jnp.where(j > jd, jnp.float32(-1.0), jnp.float32(1.0))
        beta = (((j * tk).astype(F32) - qpos) * sigma) * sgn
        return st, beta

    def consume(n, s_ref, slot):
        start = pl.multiple_of(block_of(n) * tk, tk)
        beta = beta_ref[pl.ds(slot, 1), :]
        m_prev = m_ref[...]
        m_new = jnp.maximum(m_prev, mx_ref[pl.ds(slot, 1), :])
        alpha = jnp.exp2(m_prev - m_new)
        p = jnp.exp2(s_ref[...] - (m_new - beta)).astype(BF16)
        vt = jnp.concatenate([vt_ref[:, pl.ds(start, tk)], ones_rows], axis=0)
        acc_ref[...] = alpha * acc_ref[...] + jnp.dot(vt, p, preferred_element_type=F32)
        m_ref[...] = m_new

    def produce(n, s_ref, slot):
        st, beta = scores(block_of(jnp.minimum(n, nk - 1)))
        s_ref[...] = st
        mx_ref[pl.ds(slot, 1), :] = jnp.max(st, axis=0, keepdims=True) + beta
        beta_ref[pl.ds(slot, 1), :] = beta

    st, beta = scores(jd)
    kk = lax.broadcasted_iota(jnp.int32, (tk, tq), 0)
    qq = lax.broadcasted_iota(jnp.int32, (tk, tq), 1)
    fix = (jnp.maximum(kk - qq + (jd * tk - q0), 0).astype(F32) * sigma) * -2.0
    st = st + jnp.concatenate([fix, fix], axis=1)
    s0_ref[...] = st
    mx_ref[pl.ds(0, 1), :] = jnp.max(st, axis=0, keepdims=True) + beta
    beta_ref[pl.ds(0, 1), :] = beta
    m_ref[...] = jnp.full(m_ref.shape, -jnp.inf, F32)
    acc_ref[...] = jnp.zeros(acc_ref.shape, F32)

    def pair(t, carry):
        n = 2 * t
        produce(n + 1, s1_ref, 1)
        consume(n, s0_ref, 0)
        produce(n + 2, s0_ref, 0)
        consume(n + 1, s1_ref, 1)
        return carry

    lax.fori_loop(0, nk // 2, pair, 0)

    lam = (jnp.exp(jnp.sum(lq1_ref[...] * lk1_ref[...], axis=-1, keepdims=True))
           - jnp.exp(jnp.sum(lq2_ref[...] * lk2_ref[...], axis=-1, keepdims=True))
           + LAM_INIT)
    acc = acc_ref[...]
    o = acc[:HEAD_WIDTH, :] / acc[HEAD_WIDTH:HEAD_WIDTH + 1, :]
    at = o[:, :tq] - lam * o[:, tq:]
    ms = jnp.mean(at * at, axis=0, keepdims=True)
    at = at * lax.rsqrt(ms + SUBLN_EPS) * g_ref[...] * (1.0 - LAM_INIT)
    o_ref[0] = at.T.astype(o_ref.dtype)


def _attention(q, k, vt, lq1, lk1, lq2, lk2, g_subln_col, tq, tk):
    b, s, _ = q.shape
    nk = s // tk
    assert tk % tq == 0 and nk % 2 == 0 and nk >= 2
    vec = lambda n: pl.BlockSpec((1, n), lambda bi, hi, qi: (0, 0))
    return pl.pallas_call(
        functools.partial(_attn_body, tq=tq, tk=tk, nk=nk),
        grid=(b, N_HEADS, s // tq),
        in_specs=[pl.BlockSpec((1, tq, HEAD_WIDTH), lambda bi, hi, qi: (bi, qi, hi)),
                  pl.BlockSpec((1, s, HEAD_WIDTH), lambda bi, hi, qi: (bi, 0, hi)),
                  pl.BlockSpec((HEAD_WIDTH, s), lambda bi, hi, qi: (hi, bi)),
                  vec(HEAD_DIM), vec(HEAD_DIM), vec(HEAD_DIM), vec(HEAD_DIM),
                  pl.BlockSpec((HEAD_WIDTH, 1), lambda bi, hi, qi: (0, 0))],
        out_specs=pl.BlockSpec((1, tq, HEAD_WIDTH), lambda bi, hi, qi: (bi, qi, hi)),
        out_shape=jax.ShapeDtypeStruct((b, s, ATTN_WIDTH), BF16),
        scratch_shapes=[pltpu.VMEM((2 * tq, 2 * HEAD_WIDTH), BF16),
                        pltpu.VMEM((2, tk, HEAD_WIDTH), BF16),
                        pltpu.VMEM((tk, 2 * tq), F32),
                        pltpu.VMEM((tk, 2 * tq), F32),
                        pltpu.VMEM((2, 2 * tq), F32),
                        pltpu.VMEM((2, 2 * tq), F32),
                        pltpu.VMEM((1, 2 * tq), F32),
                        pltpu.VMEM((HEAD_WIDTH + V7X_BF16_ROWS, 2 * tq), F32)],
        compiler_params=_params(("parallel", "parallel", "parallel")),
        name="diff_attention",
    )(q, k, vt, lq1, lk1, lq2, lk2, g_subln_col)


def _cpow(lre, lim, n):
    shape = jnp.broadcast_shapes(lre.shape, n.shape)
    pre = jnp.ones(shape, F32)
    pim = jnp.zeros(shape, F32)
    bre, bim = lre, lim
    for bit in range(POW_BITS):
        on = ((n >> bit) & 1) == 1
        mre = jnp.where(on, bre, 1.0)
        mim = jnp.where(on, bim, 0.0)
        pre, pim = pre * mre - pim * mim, pre * mim + pim * mre
        bre, bim = bre * bre - bim * bim, 2.0 * bre * bim
    return pre, pim


def _discretise(a_re, a_im, log_step):
    step = jnp.exp(log_step)
    mag = jnp.exp(a_re * step)
    lre = mag * jnp.cos(a_im * step)
    lim = mag * jnp.sin(a_im * step)
    den = a_re * a_re + a_im * a_im
    nr = lre - 1.0
    fre = (nr * a_re + lim * a_im) / den
    fim = (lim * a_re - nr * a_im) / den
    return lre, lim, fre, fim


def _ssm_weights_body(arow_re, arow_im, acol_re, acol_im, lsrow, lscol, bt_re, bt_im,
                      ct_re, ct_im, wt_ref, ws_ref, wo_ref, at_ref):
    t, cw = SSM_CHUNK, CHUNK_W
    lane_pair = lax.broadcasted_iota(jnp.int32, (1, PAIR_STATE), 1) // STATE
    row_pair = lax.broadcasted_iota(jnp.int32, (PAIR_STATE, 1), 0) // STATE

    ch = lax.broadcasted_iota(jnp.int32, (SSM_GROUP, LAG_W), 0)
    ln = lax.broadcasted_iota(jnp.int32, (SSM_GROUP, LAG_W), 1)
    tile_lag = jnp.where((ln % SSM_GROUP) == ch, 1.0, 0.0).astype(F32)

    step_rows = lax.broadcasted_iota(jnp.int32, (cw, 1), 0) // SSM_GROUP
    slot = lax.broadcasted_iota(jnp.int32, (1, LAG_W), 1) // SSM_GROUP
    step_lanes = lax.broadcasted_iota(jnp.int32, (1, cw), 1) // SSM_GROUP

    zt = [jnp.zeros((SSM_GROUP, LAG_W), F32) for _ in range(2)]
    for d in range(2):
        lre, lim, fre, fim = _discretise(arow_re[d, 0], arow_im[d, 0], lsrow[d, 0])
        bre, bim = bt_re[d, 0], bt_im[d, 0]
        bbre = fre * bre - fim * bim
        bbim = fre * bim + fim * bre
        expo = (t - 1 - step_rows) if d == 0 else step_rows
        pre, pim = _cpow(lre, lim, expo)
        tbre = jnp.tile(bbre, (t, 1))
        tbim = jnp.tile(bbim, (t, 1))
        inj_re = pre * tbre - pim * tbim
        inj_im = pre * tbim + pim * tbre
        for g2 in range(2):
            keep = lane_pair == g2
            rows = pl.ds(g2 * cw, cw)
            ws_ref[0, rows, pl.ds((2 * d) * PAIR_STATE, PAIR_STATE)] = (
                jnp.where(keep, inj_re, 0.0).astype(ws_ref.dtype))
            ws_ref[0, rows, pl.ds((2 * d + 1) * PAIR_STATE, PAIR_STATE)] = (
                jnp.where(keep, inj_im, 0.0).astype(ws_ref.dtype))
        dre, dim_ = _cpow(lre, lim, jnp.full((1, 1), t, jnp.int32))
        at_ref[0, pl.ds(2 * d, 1), :] = dre
        at_ref[0, pl.ds(2 * d + 1, 1), :] = dim_

        cre, cim, _, _ = _discretise(acol_re[d, 0], acol_im[d, 0], lscol[d, 0])
        ctl_re = jnp.dot(ct_re[d, 0], tile_lag, precision=HI, preferred_element_type=F32)
        ctl_im = jnp.dot(ct_im[d, 0], tile_lag, precision=HI, preferred_element_type=F32)
        lag = (slot - (t - 1)) if d == 0 else ((t - 1) - slot)
        valid = (lag >= 0) & (slot < 2 * t - 1)
        qre, qim = _cpow(cre, cim, jnp.maximum(lag, 0))
        r_re = jnp.where(valid, qre * ctl_re - qim * ctl_im, 0.0)
        r_im = jnp.where(valid, qre * ctl_im + qim * ctl_re, 0.0)
        for g2 in range(2):
            keep = lane_pair == g2
            zt[g2] = (zt[g2]
                      + jnp.dot(jnp.where(keep, bbre, 0.0), r_re, precision=HI,
                                preferred_element_type=F32)
                      - jnp.dot(jnp.where(keep, bbim, 0.0), r_im, precision=HI,
                                preferred_element_type=F32))

        out_pow = (step_lanes + 1) if d == 0 else (t - step_lanes)
        ore, oim = _cpow(cre, cim, out_pow)
        c_re = ctl_re[:, :cw]
        c_im = ctl_im[:, :cw]
        rd_re = ore * c_re - oim * c_im
        rd_im = -(ore * c_im + oim * c_re)
        for g2 in range(2):
            keep = row_pair == g2
            cols = pl.ds(g2 * cw, cw)
            wo_ref[0, pl.ds((2 * d) * PAIR_STATE, PAIR_STATE), cols] = (
                jnp.where(keep, rd_re, 0.0).astype(wo_ref.dtype))
            wo_ref[0, pl.ds((2 * d + 1) * PAIR_STATE, PAIR_STATE), cols] = (
                jnp.where(keep, rd_im, 0.0).astype(wo_ref.dtype))

    for g2 in range(2):
        for s in range(t):
            off = (t - 1 - s) * SSM_GROUP
            shifted = zt[g2] if off == 0 else pltpu.roll(zt[g2], LAG_W - off, 1)
            wt_ref[0, g2, pl.ds(s * SSM_GROUP, SSM_GROUP), :] = (
                shifted[:, :cw].astype(wt_ref.dtype))


def _ssm_weights(a_re, a_im, log_step, b_re, b_im, c_re, c_im):
    np_, g2p = N_GROUP_PAIRS, PAIR_STATE
    arow = lambda a: a.reshape(2, np_, 1, g2p)
    acol = lambda a: a.reshape(2, np_, g2p, 1)
    ls = jnp.repeat(log_step, STATE, axis=-1)
    bt = lambda w: (w.reshape(2, np_, 2, STATE, SSM_GROUP)
                    .transpose(0, 1, 4, 2, 3).reshape(2, np_, SSM_GROUP, g2p))
    ct = lambda w: (w.reshape(2, np_, 2, SSM_GROUP, STATE)
                    .transpose(0, 1, 2, 4, 3).reshape(2, np_, g2p, SSM_GROUP))
    ins = [arow(a_re), arow(a_im), acol(a_re), acol(a_im), arow(ls), acol(ls),
           bt(b_re), bt(b_im), ct(c_re), ct(c_im)]
    spec = lambda x: pl.BlockSpec((2, 1) + x.shape[2:], lambda k: (0, k, 0, 0))
    return pl.pallas_call(
        _ssm_weights_body,
        grid=(np_,),
        in_specs=[spec(x) for x in ins],
        out_specs=[pl.BlockSpec((1, 2, CHUNK_W, CHUNK_W), lambda k: (k, 0, 0, 0)),
                   pl.BlockSpec((1, 2 * CHUNK_W, N_SLABS * g2p), lambda k: (k, 0, 0)),
                   pl.BlockSpec((1, N_SLABS * g2p, 2 * CHUNK_W), lambda k: (k, 0, 0)),
                   pl.BlockSpec((1, N_SLABS, g2p), lambda k: (k, 0, 0))],
        out_shape=[jax.ShapeDtypeStruct((np_, 2, CHUNK_W, CHUNK_W), BF16),
                   jax.ShapeDtypeStruct((np_, 2 * CHUNK_W, N_SLABS * g2p), BF16),
                   jax.ShapeDtypeStruct((np_, N_SLABS * g2p, 2 * CHUNK_W), BF16),
                   jax.ShapeDtypeStruct((np_, N_SLABS, g2p), F32)],
        compiler_params=_params(("parallel",)),
        name="ssm_weights",
    )(*ins)


def _ssm_inject_body(x_ref, ws_ref, s_ref):
    s = jnp.dot(x_ref[0], ws_ref[0], preferred_element_type=F32)
    for slab in range(N_SLABS):
        s_ref[slab] = s[:, slab * PAIR_STATE:(slab + 1) * PAIR_STATE]


def _ssm_inject(x, ws):
    np_, r, _ = x.shape
    return pl.pallas_call(
        _ssm_inject_body,
        grid=(np_,),
        in_specs=[pl.BlockSpec((1, r, 2 * CHUNK_W), lambda k: (k, 0, 0)),
                  pl.BlockSpec((1,) + ws.shape[1:], lambda k: (k, 0, 0))],
        out_specs=pl.BlockSpec((N_SLABS, r, PAIR_STATE), lambda k: (0, 0, k)),
        out_shape=jax.ShapeDtypeStruct((N_SLABS, r, np_ * PAIR_STATE), F32),
        compiler_params=_params(("parallel",)),
        name="ssm_inject",
    )(x, ws)


def _ssm_scan_body(s_ref, at_ref, hin_ref, *, batch, n_tiles):
    sub = V7X_SUBLANES // batch
    lanes = s_ref.shape[-1]
    a = [at_ref[pl.ds(slab, 1), :] for slab in range(N_SLABS)]

    def sweep(d, tile, carry):
        hre, him = carry
        are, aim = a[2 * d], a[2 * d + 1]
        row0 = pl.multiple_of(tile * V7X_SUBLANES, V7X_SUBLANES)
        sre = s_ref[2 * d, pl.ds(row0, V7X_SUBLANES), :]
        sim = s_ref[2 * d + 1, pl.ds(row0, V7X_SUBLANES), :]
        out_re = [None] * sub
        out_im = [None] * sub
        order = range(sub) if d == 0 else range(sub - 1, -1, -1)
        for q in order:
            out_re[q], out_im[q] = hre, him
            xr = sre[q * batch:(q + 1) * batch]
            xi = sim[q * batch:(q + 1) * batch]
            hre, him = are * hre - aim * him + xr, are * him + aim * hre + xi
        hin_ref[2 * d, pl.ds(row0, V7X_SUBLANES), :] = jnp.concatenate(out_re, axis=0)
        hin_ref[2 * d + 1, pl.ds(row0, V7X_SUBLANES), :] = jnp.concatenate(out_im, axis=0)
        return hre, him

    zero = (jnp.zeros((batch, lanes), F32), jnp.zeros((batch, lanes), F32))
    lax.fori_loop(0, n_tiles, lambda i, c: sweep(0, i, c), zero)
    lax.fori_loop(0, n_tiles, lambda i, c: sweep(1, n_tiles - 1 - i, c), zero)


def _ssm_scan(s, at, batch, lane_block):
    _, r, lanes = s.shape
    assert V7X_SUBLANES % batch == 0 and r % V7X_SUBLANES == 0
    return pl.pallas_call(
        functools.partial(_ssm_scan_body, batch=batch, n_tiles=r // V7X_SUBLANES),
        grid=(lanes // lane_block,),
        in_specs=[pl.BlockSpec((N_SLABS, r, lane_block), lambda k: (0, 0, k)),
                  pl.BlockSpec((N_SLABS, lane_block), lambda k: (0, k))],
        out_specs=pl.BlockSpec((N_SLABS, r, lane_block), lambda k: (0, 0, k)),
        out_shape=jax.ShapeDtypeStruct(s.shape, F32),
        compiler_params=_params(("parallel",)),
        name="ssm_scan",
    )(s, at)


def _ssm_mix_body(x_ref, wt_ref, hin_ref, wo_ref, y_ref):
    x = x_ref[0]
    hin = jnp.concatenate([hin_ref[slab] for slab in range(N_SLABS)], axis=1).astype(BF16)
    carried = jnp.dot(hin, wo_ref[0], preferred_element_type=F32)
    for g2 in range(2):
        cols = slice(g2 * CHUNK_W, (g2 + 1) * CHUNK_W)
        y_ref[0, :, cols] = carried[:, cols] + jnp.dot(x[:, cols], wt_ref[0, g2],
                                                       preferred_element_type=F32)


def _ssm_mix(x, wt, hin, wo):
    np_, r, _ = x.shape
    return pl.pallas_call(
        _ssm_mix_body,
        grid=(np_,),
        in_specs=[pl.BlockSpec((1, r, 2 * CHUNK_W), lambda k: (k, 0, 0)),
                  pl.BlockSpec((1,) + wt.shape[1:], lambda k: (k, 0, 0, 0)),
                  pl.BlockSpec((N_SLABS, r, PAIR_STATE), lambda k: (0, 0, k)),
                  pl.BlockSpec((1,) + wo.shape[1:], lambda k: (k, 0, 0))],
        out_specs=pl.BlockSpec((1, r, 2 * CHUNK_W), lambda k: (k, 0, 0)),
        out_shape=jax.ShapeDtypeStruct((np_, r, 2 * CHUNK_W), F32),
        compiler_params=_params(("parallel",)),
        name="ssm_mix",
    )(x, wt, hin, wo)


def _outproj_body(x_ref, a_ref, y_ref, u_ref, d_ref, wg_ref, bg_ref, gs_ref, wo_ref, gf_ref,
                  x1_ref, h_ref):
    y = jax.nn.gelu(y_ref[...] + d_ref[...] * u_ref[...])
    gate = jnp.dot(y.astype(BF16), wg_ref[...], preferred_element_type=F32) + bg_ref[...]
    s = _rms(y * jax.nn.sigmoid(gate), gs_ref[...], NORM_EPS)
    mixed = (jnp.dot(a_ref[...], wo_ref[:ATTN_WIDTH, :], preferred_element_type=F32)
             + jnp.dot(s.astype(BF16), wo_ref[ATTN_WIDTH:, :], preferred_element_type=F32))
    x1 = x_ref[...] + mixed
    x1_ref[...] = x1
    h_ref[...] = _rms(x1, gf_ref[...], NORM_EPS).astype(h_ref.dtype)


def _outproj(x2, a2, y2, u2, d, w_glu, b_glu, g_ssm, w_out, g_ffn, tm):
    n_tok = x2.shape[0]
    row = lambda w: pl.BlockSpec((tm, w), lambda i: (i, 0))
    full = lambda x: pl.BlockSpec(x.shape, lambda i: (0, 0))
    return pl.pallas_call(
        _outproj_body,
        grid=(n_tok // tm,),
        in_specs=[row(D_MODEL), row(ATTN_WIDTH), row(SSM_WIDTH), row(SSM_WIDTH),
                  full(d), full(w_glu), full(b_glu), full(g_ssm), full(w_out), full(g_ffn)],
        out_specs=[row(D_MODEL), row(D_MODEL)],
        out_shape=[jax.ShapeDtypeStruct((n_tok, D_MODEL), F32),
                   jax.ShapeDtypeStruct((n_tok, D_MODEL), BF16)],
        compiler_params=_params(("parallel",)),
        name="outproj",
    )(x2, a2, y2, u2, d, w_glu, b_glu, g_ssm, w_out, g_ffn)


def _ffn_body(h_ref, hp_ref, hn_ref, x1_ref, wu_ref, cw_ref, cb_ref, wd_ref, gf_ref, o_ref,
              hcat_ref, *, tm, tiles_per_seq, ff_chunk):
    i = pl.program_id(0)
    halo = V7X_BF16_ROWS
    first = (i % tiles_per_seq) == 0
    last = (i % tiles_per_seq) == tiles_per_seq - 1
    zeros = jnp.zeros((halo, D_MODEL), hcat_ref.dtype)
    hcat_ref[:halo, :] = jnp.where(first, zeros, hp_ref[...])
    hcat_ref[halo:halo + tm, :] = h_ref[...]
    hcat_ref[halo + tm:, :] = jnp.where(last, zeros, hn_ref[...])
    hcat = hcat_ref[...]
    rows = tm + 2 * halo

    def conv(z, col0):
        cols = pl.ds(col0, ff_chunk)
        prev = pltpu.roll(z, 1, 0)[halo:halo + tm]
        nxt = pltpu.roll(z, rows - 1, 0)[halo:halo + tm]
        return (prev * cw_ref[pl.ds(0, 1), cols] + z[halo:halo + tm] * cw_ref[pl.ds(1, 1), cols]
                + nxt * cw_ref[pl.ds(2, 1), cols] + cb_ref[:, cols])

    acc = jnp.zeros((tm, D_MODEL), F32)
    for c in range(D_FF // ff_chunk):
        g0 = c * ff_chunk
        v0 = D_FF + c * ff_chunk
        zg = jnp.dot(hcat, wu_ref[:, pl.ds(g0, ff_chunk)], preferred_element_type=F32)
        zv = jnp.dot(hcat, wu_ref[:, pl.ds(v0, ff_chunk)], preferred_element_type=F32)
        act = (jax.nn.gelu(conv(zg, g0)) * conv(zv, v0)).astype(BF16)
        acc = acc + jnp.dot(act, wd_ref[pl.ds(g0, ff_chunk), :], preferred_element_type=F32)
    o_ref[...] = _rms(x1_ref[...] + acc, gf_ref[...], NORM_EPS)


def _ffn(h2, x1, w_up, conv_w, conv_b, w_down, g_final, tm, seq, ff_chunk):
    n_tok = h2.shape[0]
    halo = V7X_BF16_ROWS
    per = tm // halo
    n_halo = n_tok // halo
    row = lambda w: pl.BlockSpec((tm, w), lambda i: (i, 0))
    once = lambda x: pl.BlockSpec(x.shape, lambda i: (0, 0), pipeline_mode=pl.Buffered(1))
    return pl.pallas_call(
        functools.partial(_ffn_body, tm=tm, tiles_per_seq=seq // tm, ff_chunk=ff_chunk),
        grid=(n_tok // tm,),
        in_specs=[row(D_MODEL),
                  pl.BlockSpec((halo, D_MODEL), lambda i: (jnp.maximum(i * per - 1, 0), 0)),
                  pl.BlockSpec((halo, D_MODEL),
                               lambda i: (jnp.minimum((i + 1) * per, n_halo - 1), 0)),
                  row(D_MODEL), once(w_up), once(conv_w), once(conv_b), once(w_down),
                  once(g_final)],
        out_specs=row(D_MODEL),
        out_shape=jax.ShapeDtypeStruct((n_tok, D_MODEL), F32),
        scratch_shapes=[pltpu.VMEM((tm + 2 * halo, D_MODEL), BF16)],
        compiler_params=_params(("parallel",)),
        name="ffn",
    )(h2, h2, h2, x1, w_up, conv_w, conv_b, w_down, g_final)


def _tile(pref, n):
    t = min(pref, n)
    assert n % t == 0, (pref, n)
    return t


def _to_chunks(u, batch, seq):
    nc = seq // SSM_CHUNK
    x = u.astype(BF16).reshape(batch, nc, SSM_CHUNK, N_GROUP_PAIRS, 2, SSM_GROUP)
    return x.transpose(3, 1, 0, 4, 2, 5).reshape(N_GROUP_PAIRS, nc * batch, 2 * CHUNK_W)


def _from_chunks(y, batch, seq):
    nc = seq // SSM_CHUNK
    y = y.reshape(N_GROUP_PAIRS, nc, batch, 2, SSM_CHUNK, SSM_GROUP)
    return y.transpose(2, 1, 4, 0, 3, 5).reshape(batch * seq, SSM_WIDTH)


def _trunk(x, p, ssm_w):
    batch, seq, _ = x.shape
    n_tok = batch * seq
    x2 = x.reshape(n_tok, D_MODEL)
    tm = _tile(512, seq)

    q, k, vt, u = _inproj(x2, p["g_mix"], p["w_qku"], p["w_vt"], tm)

    shape3 = (batch, seq, ATTN_WIDTH)
    a = _attention(q.reshape(shape3), k.reshape(shape3), vt,
                   p["lq1"], p["lk1"], p["lq2"], p["lk2"], p["g_subln"],
                   _tile(256, seq), _tile(512, seq))

    wt, ws, wo, at = ssm_w
    xc = _to_chunks(u, batch, seq)
    s_in = _ssm_inject(xc, ws)
    at_flat = at.transpose(1, 0, 2).reshape(N_SLABS, N_GROUP_PAIRS * PAIR_STATE)
    hin = _ssm_scan(s_in, at_flat, batch, 4 * PAIR_STATE)
    y = _from_chunks(_ssm_mix(xc, wt, hin, wo), batch, seq)

    x1, h = _outproj(x2, a.reshape(n_tok, ATTN_WIDTH), y, u, p["d"], p["w_glu"], p["b_glu"],
                     p["g_ssm"], p["w_out"], p["g_ffn"], tm)
    out = _ffn(h, x1, p["w_up"], p["conv_w"], p["conv_b"], p["w_down"], p["g_final"],
               tm, seq, 256)
    return out.reshape(batch, seq, D_MODEL)


def kernel(x_prompt, x_sample, g_mix_norm, w_in, lambda_q1, lambda_k1, lambda_q2, lambda_k2, g_subln, ssm_a_re, ssm_a_im, ssm_log_step, ssm_b_re, ssm_b_im, ssm_c_re, ssm_c_im, ssm_d, w_glu, b_glu, g_ssm_out, w_out, g_ffn_norm, w_up, conv_w, conv_b, w_down, g_final):
    layer = 0
    vec = lambda t: t.reshape(1, -1).astype(F32)
    p = {
        "g_mix": vec(g_mix_norm[layer]),
        "w_qku": jnp.concatenate([w_in[layer][:, :2 * ATTN_WIDTH],
                                  w_in[layer][:, 3 * ATTN_WIDTH:]], axis=1).astype(BF16),
        "w_vt": w_in[layer][:, 2 * ATTN_WIDTH:3 * ATTN_WIDTH].T.astype(BF16),
        "lq1": vec(lambda_q1[layer]), "lk1": vec(lambda_k1[layer]),
        "lq2": vec(lambda_q2[layer]), "lk2": vec(lambda_k2[layer]),
        "g_subln": g_subln[layer].reshape(-1, 1).astype(F32), "d": vec(ssm_d[layer]),
        "w_glu": w_glu[layer].astype(BF16), "b_glu": vec(b_glu[layer]),
        "g_ssm": vec(g_ssm_out[layer]), "w_out": w_out[layer].astype(BF16),
        "g_ffn": vec(g_ffn_norm[layer]), "w_up": w_up[layer].astype(BF16),
        "conv_w": conv_w[layer].astype(F32), "conv_b": vec(conv_b[layer]),
        "w_down": w_down[layer].astype(BF16), "g_final": vec(g_final),
    }
    f32 = lambda t: t[layer].astype(F32)
    ssm_w = _ssm_weights(f32(ssm_a_re), f32(ssm_a_im), f32(ssm_log_step), f32(ssm_b_re),
                         f32(ssm_b_im), f32(ssm_c_re), f32(ssm_c_im))
    return _trunk(x_prompt, p, ssm_w), _trunk(x_sample, p, ssm_w)
```

```python
import functools
import math

import jax
import jax.numpy as jnp
from jax import lax
from jax.experimental import pallas as pl
from jax.experimental.pallas import tpu as pltpu

F32 = jnp.float32
BF16 = jnp.bfloat16

D_MODEL = 1024
ATTN_WIDTH = 512
SSM_WIDTH = 512
N_HEADS = 4
HEAD_DIM = 64
HEAD_WIDTH = 2 * HEAD_DIM
SSM_GROUP = 16
N_SSM_GROUPS = 32
N_GROUP_PAIRS = N_SSM_GROUPS // 2
STATE = 64
PAIR_STATE = 2 * STATE
D_FF = 2816
NORM_EPS = 1e-6
SUBLN_EPS = 1e-5
LAM_INIT = 0.8 - 0.6 * math.exp(-0.3 * 0)
LOG2E = math.log2(math.e)
N_BIAS_LANES = 3
ATTN_PAIRS_PER_TRIP = 2

SSM_CHUNK = 32
CHUNK_W = SSM_CHUNK * SSM_GROUP
LAG_W = 2 * CHUNK_W
POW_BITS = SSM_CHUNK.bit_length()
N_SLABS = 4

V7X_MXU_WIDTH = 256
V7X_SUBLANES = 8
V7X_BF16_ROWS = 16
V7X_VMEM_LIMIT = 56 * 1024 * 1024

HI = lax.Precision.HIGHEST


def _params(sem, vmem=V7X_VMEM_LIMIT, flags=None):
    return pltpu.CompilerParams(dimension_semantics=sem, vmem_limit_bytes=vmem, flags=flags)


def _rms(x, g, eps):
    return x * lax.rsqrt(jnp.mean(x * x, axis=-1, keepdims=True) + eps) * g


def _inproj_body(x_ref, g_ref, wqku_ref, wvt_ref, q_ref, k_ref, vt_ref, u_ref):
    n = _rms(x_ref[...], g_ref[...], NORM_EPS).astype(BF16)
    proj = jnp.dot(n, wqku_ref[...], preferred_element_type=F32)
    a = ATTN_WIDTH
    q_ref[...] = (proj[:, :a] * (LOG2E * HEAD_DIM ** -0.5)).astype(BF16)
    k_ref[...] = proj[:, a:2 * a].astype(BF16)
    u_ref[...] = proj[:, 2 * a:]
    vt_ref[...] = lax.dot_general(wvt_ref[...], n, (((1,), (1,)), ((), ())),
                                  preferred_element_type=F32).astype(BF16)


def _inproj(x2, g, w_qku, w_vt, tm):
    n_tok = x2.shape[0]
    row = lambda w: pl.BlockSpec((tm, w), lambda i: (i, 0))
    full = lambda x: pl.BlockSpec(x.shape, lambda i: (0, 0))
    return pl.pallas_call(
        _inproj_body,
        grid=(n_tok // tm,),
        in_specs=[row(D_MODEL), full(g), full(w_qku), full(w_vt)],
        out_specs=[row(ATTN_WIDTH), row(ATTN_WIDTH),
                   pl.BlockSpec((ATTN_WIDTH, tm), lambda i: (0, i)), row(SSM_WIDTH)],
        out_shape=[jax.ShapeDtypeStruct((n_tok, ATTN_WIDTH), BF16),
                   jax.ShapeDtypeStruct((n_tok, ATTN_WIDTH), BF16),
                   jax.ShapeDtypeStruct((ATTN_WIDTH, n_tok), BF16),
                   jax.ShapeDtypeStruct((n_tok, SSM_WIDTH), F32)],
        compiler_params=_params(("parallel",)),
        name="inproj",
    )(x2, g, w_qku, w_vt)


def _attn_body(q_ref, k_ref, vt_ref, lq1_ref, lk1_ref, lq2_ref, lk2_ref, g_ref, o_ref,
               qa_ref, kaug_ref, s_ref, p_ref, mx_ref, beta_ref, alpha_ref, m_ref, acc_ref,
               *, tq, tk, nk):
    h = pl.program_id(1)
    i = pl.program_id(2)
    tq2 = 2 * tq
    sigma = jnp.float32(0.0)
    for head in range(N_HEADS):
        sigma = jnp.where(h == head,
                          jnp.float32(LOG2E * 2.0 ** (-8.0 * (head + 1) / N_HEADS)), sigma)
    q0 = i * tq
    jd = q0 // tk

    q = q_ref[0]
    lane = lax.broadcasted_iota(jnp.int32, q.shape, 1)
    zero = jnp.zeros_like(q)
    ones3 = jnp.where(lane < N_BIAS_LANES, 1.0, 0.0).astype(BF16)
    qa_ref[:tq, :HEAD_WIDTH] = jnp.where(lane < HEAD_DIM, q, zero)
    qa_ref[tq:, :HEAD_WIDTH] = jnp.where(lane < HEAD_DIM, zero, q)
    qa_ref[:tq, HEAD_WIDTH:] = ones3
    qa_ref[tq:, HEAD_WIDTH:] = ones3

    @pl.when(i == 0)
    def _():
        koff = lax.broadcasted_iota(jnp.int32, (tk, HEAD_WIDTH), 0).astype(F32) * sigma
        klane = lax.broadcasted_iota(jnp.int32, (tk, HEAD_WIDTH), 1)
        hi = koff.astype(BF16).astype(F32)
        mid = (koff - hi).astype(BF16).astype(F32)
        lo = koff - hi - mid
        ktile = jnp.where(klane == 0, hi,
                          jnp.where(klane == 1, mid, jnp.where(klane == 2, lo, 0.0)))
        kaug_ref[0] = ktile.astype(BF16)
        kaug_ref[1] = (-ktile).astype(BF16)

    ql = lax.broadcasted_iota(jnp.int32, (1, tq2), 1)
    qpos = (q0 + jnp.where(ql < tq, ql, ql - tq)).astype(F32)
    ones_rows = jnp.where(
        lax.broadcasted_iota(jnp.int32, (V7X_BF16_ROWS, tk), 0) == 0, 1.0, 0.0).astype(BF16)

    def block_of(n):
        return jnp.where(n == 0, jd, jnp.where(n - 1 < jd, n - 1, n))

    lane_chunks = [pl.ds(c * V7X_MXU_WIDTH, V7X_MXU_WIDTH) for c in range(tq2 // V7X_MXU_WIDTH)]
    kk = lax.broadcasted_iota(jnp.int32, (tk, V7X_MXU_WIDTH), 0)
    qq = lax.broadcasted_iota(jnp.int32, (tk, V7X_MXU_WIDTH), 1)

    def keys_of(n):
        j = block_of(n)
        right = (j > jd).astype(jnp.int32)
        start = pl.multiple_of(j * tk, tk)
        ka = jnp.concatenate([k_ref[0, pl.ds(start, tk), :], kaug_ref[right]], axis=1)
        sgn = jnp.where(j > jd, jnp.float32(-1.0), jnp.float32(1.0))
        beta = (((j * tk).astype(F32) - qpos) * sigma) * sgn
        return ka, beta

    def scores_chunk(ka, beta, slot, c, diagonal=False):
        cs = lane_chunks[c]
        st = lax.dot_general(ka, qa_ref[cs, :], (((1,), (1,)), ((), ())),
                             preferred_element_type=F32)
        if diagonal:
            qoff = (c * V7X_MXU_WIDTH) % tq
            st = st + (jnp.maximum(kk - qq + (jd * tk - q0 - qoff), 0).astype(F32)
                       * sigma) * -2.0
        s_ref[slot, :, cs] = st
        mx_ref[slot, :, cs] = jnp.max(st, axis=0, keepdims=True) + beta[:, c * V7X_MXU_WIDTH:
                                                                      (c + 1) * V7X_MXU_WIDTH]

    def softmax_chunk(slot, c):
        cs = lane_chunks[c]
        m_prev = m_ref[:, cs]
        m_new = jnp.maximum(m_prev, mx_ref[slot, :, cs])
        alpha_ref[slot, :, cs] = jnp.exp2(m_prev - m_new)
        p_ref[slot, :, cs] = jnp.exp2(s_ref[slot, :, cs]
                                      - (m_new - beta_ref[slot, :, cs])).astype(BF16)
        m_ref[:, cs] = m_new

    def pv_chunk(vt, slot, c):
        cs = lane_chunks[c]
        acc_ref[:, cs] = alpha_ref[slot, :, cs] * acc_ref[:, cs] + jnp.dot(
            vt, p_ref[slot, :, cs], preferred_element_type=F32)

    def values_of(n):
        start = pl.multiple_of(block_of(n) * tk, tk)
        return jnp.concatenate([vt_ref[:, pl.ds(start, tk)], ones_rows], axis=0)

    def pipeline_step(n_scores, n_softmax, n_pv, diagonal=False):
        if n_scores is not None:
            ka, beta = keys_of(n_scores[0])
            beta_ref[n_scores[1]] = beta
        if n_pv is not None:
            vt = values_of(n_pv[0])
        for c in range(len(lane_chunks)):
            if n_softmax is not None:
                softmax_chunk(n_softmax, c)
            if n_scores is not None:
                scores_chunk(ka, beta, n_scores[1], c, diagonal)
            if n_pv is not None:
                pv_chunk(vt, n_pv[1], c)

    m_ref[...] = jnp.full(m_ref.shape, -jnp.inf, F32)
    acc_ref[...] = jnp.zeros(acc_ref.shape, F32)
    pipeline_step((0, 0), None, None, diagonal=True)
    pipeline_step((1, 1), 0, None)

    def steady(n, pairs):
        for k in range(pairs):
            pipeline_step((n + 2 * k + 2, 0), 1, (n + 2 * k, 0))
            pipeline_step((n + 2 * k + 3, 1), 0, (n + 2 * k + 1, 1))

    n_pairs = nk // 2 - 1
    trips = n_pairs // ATTN_PAIRS_PER_TRIP

    def trip(t, carry):
        steady(2 * ATTN_PAIRS_PER_TRIP * t, ATTN_PAIRS_PER_TRIP)
        return carry

    lax.fori_loop(0, trips, trip, 0)
    if n_pairs % ATTN_PAIRS_PER_TRIP:
        steady(2 * ATTN_PAIRS_PER_TRIP * trips, n_pairs % ATTN_PAIRS_PER_TRIP)
    pipeline_step(None, 1, (nk - 2, 0))
    pipeline_step(None, None, (nk - 1, 1))

    lam = (jnp.exp(jnp.sum(lq1_ref[...] * lk1_ref[...], axis=-1, keepdims=True))
           - jnp.exp(jnp.sum(lq2_ref[...] * lk2_ref[...], axis=-1, keepdims=True))
           + LAM_INIT)
    acc = acc_ref[...]
    o = acc[:HEAD_WIDTH, :] * (1.0 / acc[HEAD_WIDTH:HEAD_WIDTH + 1, :])
    at = o[:, :tq] - lam * o[:, tq:]
    ms = jnp.mean(at * at, axis=0, keepdims=True)
    at = at * lax.rsqrt(ms + SUBLN_EPS) * g_ref[...] * (1.0 - LAM_INIT)
    o_ref[0] = at.T.astype(o_ref.dtype)


def _attention(q, k, vt, lq1, lk1, lq2, lk2, g_subln_col, tq, tk):
    b, s, _ = q.shape
    nk = s // tk
    assert tk % tq == 0 and nk % 2 == 0 and nk >= 2
    vec = lambda n: pl.BlockSpec((1, n), lambda bi, hi, qi: (0, 0))
    return pl.pallas_call(
        functools.partial(_attn_body, tq=tq, tk=tk, nk=nk),
        grid=(b, N_HEADS, s // tq),
        in_specs=[pl.BlockSpec((1, tq, HEAD_WIDTH), lambda bi, hi, qi: (bi, qi, hi)),
                  pl.BlockSpec((1, s, HEAD_WIDTH), lambda bi, hi, qi: (bi, 0, hi)),
                  pl.BlockSpec((HEAD_WIDTH, s), lambda bi, hi, qi: (hi, bi)),
                  vec(HEAD_DIM), vec(HEAD_DIM), vec(HEAD_DIM), vec(HEAD_DIM),
                  pl.BlockSpec((HEAD_WIDTH, 1), lambda bi, hi, qi: (0, 0))],
        out_specs=pl.BlockSpec((1, tq, HEAD_WIDTH), lambda bi, hi, qi: (bi, qi, hi)),
        out_shape=jax.ShapeDtypeStruct((b, s, ATTN_WIDTH), BF16),
        scratch_shapes=[pltpu.VMEM((2 * tq, 2 * HEAD_WIDTH), BF16),
                        pltpu.VMEM((2, tk, HEAD_WIDTH), BF16),
                        pltpu.VMEM((2, tk, 2 * tq), F32),
                        pltpu.VMEM((2, tk, 2 * tq), BF16),
                        pltpu.VMEM((2, 1, 2 * tq), F32),
                        pltpu.VMEM((2, 1, 2 * tq), F32),
                        pltpu.VMEM((2, 1, 2 * tq), F32),
                        pltpu.VMEM((1, 2 * tq), F32),
                        pltpu.VMEM((HEAD_WIDTH + V7X_BF16_ROWS, 2 * tq), F32)],
        compiler_params=_params(("parallel", "parallel", "arbitrary")),
        name="diff_attention",
    )(q, k, vt, lq1, lk1, lq2, lk2, g_subln_col)


def _cpow(lre, lim, n):
    shape = jnp.broadcast_shapes(lre.shape, n.shape)
    pre = jnp.ones(shape, F32)
    pim = jnp.zeros(shape, F32)
    bre, bim = lre, lim
    for bit in range(POW_BITS):
        on = ((n >> bit) & 1) == 1
        mre = jnp.where(on, bre, 1.0)
        mim = jnp.where(on, bim, 0.0)
        pre, pim = pre * mre - pim * mim, pre * mim + pim * mre
        bre, bim = bre * bre - bim * bim, 2.0 * bre * bim
    return pre, pim


def _discretise(a_re, a_im, log_step):
    step = jnp.exp(log_step)
    mag = jnp.exp(a_re * step)
    lre = mag * jnp.cos(a_im * step)
    lim = mag * jnp.sin(a_im * step)
    den = a_re * a_re + a_im * a_im
    nr = lre - 1.0
    fre = (nr * a_re + lim * a_im) / den
    fim = (lim * a_re - nr * a_im) / den
    return lre, lim, fre, fim


def _ssm_weights_body(arow_re, arow_im, acol_re, acol_im, lsrow, lscol, bt_re, bt_im,
                      ct_re, ct_im, wt_ref, ws_ref, wo_ref, at_ref):
    t, cw = SSM_CHUNK, CHUNK_W
    lane_pair = lax.broadcasted_iota(jnp.int32, (1, PAIR_STATE), 1) // STATE
    row_pair = lax.broadcasted_iota(jnp.int32, (PAIR_STATE, 1), 0) // STATE

    ch = lax.broadcasted_iota(jnp.int32, (SSM_GROUP, LAG_W), 0)
    ln = lax.broadcasted_iota(jnp.int32, (SSM_GROUP, LAG_W), 1)
    tile_lag = jnp.where((ln % SSM_GROUP) == ch, 1.0, 0.0).astype(F32)

    step_rows = lax.broadcasted_iota(jnp.int32, (cw, 1), 0) // SSM_GROUP
    slot = lax.broadcasted_iota(jnp.int32, (1, LAG_W), 1) // SSM_GROUP
    step_lanes = lax.broadcasted_iota(jnp.int32, (1, cw), 1) // SSM_GROUP

    zt = [jnp.zeros((SSM_GROUP, LAG_W), F32) for _ in range(2)]
    for d in range(2):
        lre, lim, fre, fim = _discretise(arow_re[d, 0], arow_im[d, 0], lsrow[d, 0])
        bre, bim = bt_re[d, 0], bt_im[d, 0]
        bbre = fre * bre - fim * bim
        bbim = fre * bim + fim * bre
        expo = (t - 1 - step_rows) if d == 0 else step_rows
        pre, pim = _cpow(lre, lim, expo)
        tbre = jnp.tile(bbre, (t, 1))
        tbim = jnp.tile(bbim, (t, 1))
        inj_re = pre * tbre - pim * tbim
        inj_im = pre * tbim + pim * tbre
        for g2 in range(2):
            keep = lane_pair == g2
            rows = pl.ds(g2 * cw, cw)
            ws_ref[0, rows, pl.ds((2 * d) * PAIR_STATE, PAIR_STATE)] = (
                jnp.where(keep, inj_re, 0.0).astype(ws_ref.dtype))
            ws_ref[0, rows, pl.ds((2 * d + 1) * PAIR_STATE, PAIR_STATE)] = (
                jnp.where(keep, inj_im, 0.0).astype(ws_ref.dtype))
        dre, dim_ = _cpow(lre, lim, jnp.full((1, 1), t, jnp.int32))
        at_ref[0, pl.ds(2 * d, 1), :] = dre
        at_ref[0, pl.ds(2 * d + 1, 1), :] = dim_

        cre, cim, _, _ = _discretise(acol_re[d, 0], acol_im[d, 0], lscol[d, 0])
        ctl_re = jnp.dot(ct_re[d, 0], tile_lag, precision=HI, preferred_element_type=F32)
        ctl_im = jnp.dot(ct_im[d, 0], tile_lag, precision=HI, preferred_element_type=F32)
        lag = (slot - (t - 1)) if d == 0 else ((t - 1) - slot)
        valid = (lag >= 0) & (slot < 2 * t - 1)
        qre, qim = _cpow(cre, cim, jnp.maximum(lag, 0))
        r_re = jnp.where(valid, qre * ctl_re - qim * ctl_im, 0.0)
        r_im = jnp.where(valid, qre * ctl_im + qim * ctl_re, 0.0)
        for g2 in range(2):
            keep = lane_pair == g2
            zt[g2] = (zt[g2]
                      + jnp.dot(jnp.where(keep, bbre, 0.0), r_re, precision=HI,
                                preferred_element_type=F32)
                      - jnp.dot(jnp.where(keep, bbim, 0.0), r_im, precision=HI,
                                preferred_element_type=F32))

        out_pow = (step_lanes + 1) if d == 0 else (t - step_lanes)
        ore, oim = _cpow(cre, cim, out_pow)
        c_re = ctl_re[:, :cw]
        c_im = ctl_im[:, :cw]
        rd_re = ore * c_re - oim * c_im
        rd_im = -(ore * c_im + oim * c_re)
        for g2 in range(2):
            keep = row_pair == g2
            cols = pl.ds(g2 * cw, cw)
            wo_ref[0, pl.ds((2 * d) * PAIR_STATE, PAIR_STATE), cols] = (
                jnp.where(keep, rd_re, 0.0).astype(wo_ref.dtype))
            wo_ref[0, pl.ds((2 * d + 1) * PAIR_STATE, PAIR_STATE), cols] = (
                jnp.where(keep, rd_im, 0.0).astype(wo_ref.dtype))

    for g2 in range(2):
        for s in range(t):
            off = (t - 1 - s) * SSM_GROUP
            shifted = zt[g2] if off == 0 else pltpu.roll(zt[g2], LAG_W - off, 1)
            wt_ref[0, g2, pl.ds(s * SSM_GROUP, SSM_GROUP), :] = (
                shifted[:, :cw].astype(wt_ref.dtype))


def _ssm_weights(a_re, a_im, log_step, b_re, b_im, c_re, c_im):
    np_, g2p = N_GROUP_PAIRS, PAIR_STATE
    arow = lambda a: a.reshape(2, np_, 1, g2p)
    acol = lambda a: a.reshape(2, np_, g2p, 1)
    ls = jnp.repeat(log_step, STATE, axis=-1)
    bt = lambda w: (w.reshape(2, np_, 2, STATE, SSM_GROUP)
                    .transpose(0, 1, 4, 2, 3).reshape(2, np_, SSM_GROUP, g2p))
    ct = lambda w: (w.reshape(2, np_, 2, SSM_GROUP, STATE)
                    .transpose(0, 1, 2, 4, 3).reshape(2, np_, g2p, SSM_GROUP))
    ins = [arow(a_re), arow(a_im), acol(a_re), acol(a_im), arow(ls), acol(ls),
           bt(b_re), bt(b_im), ct(c_re), ct(c_im)]
    spec = lambda x: pl.BlockSpec((2, 1) + x.shape[2:], lambda k: (0, k, 0, 0))
    return pl.pallas_call(
        _ssm_weights_body,
        grid=(np_,),
        in_specs=[spec(x) for x in ins],
        out_specs=[pl.BlockSpec((1, 2, CHUNK_W, CHUNK_W), lambda k: (k, 0, 0, 0)),
                   pl.BlockSpec((1, 2 * CHUNK_W, N_SLABS * g2p), lambda k: (k, 0, 0)),
                   pl.BlockSpec((1, N_SLABS * g2p, 2 * CHUNK_W), lambda k: (k, 0, 0)),
                   pl.BlockSpec((1, N_SLABS, g2p), lambda k: (k, 0, 0))],
        out_shape=[jax.ShapeDtypeStruct((np_, 2, CHUNK_W, CHUNK_W), BF16),
                   jax.ShapeDtypeStruct((np_, 2 * CHUNK_W, N_SLABS * g2p), BF16),
                   jax.ShapeDtypeStruct((np_, N_SLABS * g2p, 2 * CHUNK_W), BF16),
                   jax.ShapeDtypeStruct((np_, N_SLABS, g2p), F32)],
        compiler_params=_params(("parallel",)),
        name="ssm_weights",
    )(*ins)


def _ssm_inject_body(x_ref, ws_ref, s_ref):
    s = jnp.dot(x_ref[0], ws_ref[0], preferred_element_type=F32)
    for slab in range(N_SLABS):
        s_ref[slab] = s[:, slab * PAIR_STATE:(slab + 1) * PAIR_STATE]


def _ssm_inject(x, ws):
    np_, r, _ = x.shape
    return pl.pallas_call(
        _ssm_inject_body,
        grid=(np_,),
        in_specs=[pl.BlockSpec((1, r, 2 * CHUNK_W), lambda k: (k, 0, 0)),
                  pl.BlockSpec((1,) + ws.shape[1:], lambda k: (k, 0, 0))],
        out_specs=pl.BlockSpec((N_SLABS, r, PAIR_STATE), lambda k: (0, 0, k)),
        out_shape=jax.ShapeDtypeStruct((N_SLABS, r, np_ * PAIR_STATE), F32),
        compiler_params=_params(("parallel",)),
        name="ssm_inject",
    )(x, ws)


def _ssm_scan_body(s_ref, at_ref, hin_ref, *, batch, n_tiles):
    sub = V7X_SUBLANES // batch
    lanes = s_ref.shape[-1]
    a = [at_ref[pl.ds(slab, 1), :] for slab in range(N_SLABS)]

    def sweep(d, tile, carry):
        hre, him = carry
        are, aim = a[2 * d], a[2 * d + 1]
        row0 = pl.multiple_of(tile * V7X_SUBLANES, V7X_SUBLANES)
        sre = s_ref[2 * d, pl.ds(row0, V7X_SUBLANES), :]
        sim = s_ref[2 * d + 1, pl.ds(row0, V7X_SUBLANES), :]
        out_re = [None] * sub
        out_im = [None] * sub
        order = range(sub) if d == 0 else range(sub - 1, -1, -1)
        for q in order:
            out_re[q], out_im[q] = hre, him
            xr = sre[q * batch:(q + 1) * batch]
            xi = sim[q * batch:(q + 1) * batch]
            hre, him = are * hre - aim * him + xr, are * him + aim * hre + xi
        hin_ref[2 * d, pl.ds(row0, V7X_SUBLANES), :] = jnp.concatenate(out_re, axis=0)
        hin_ref[2 * d + 1, pl.ds(row0, V7X_SUBLANES), :] = jnp.concatenate(out_im, axis=0)
        return hre, him

    zero = (jnp.zeros((batch, lanes), F32), jnp.zeros((batch, lanes), F32))
    lax.fori_loop(0, n_tiles, lambda i, c: sweep(0, i, c), zero)
    lax.fori_loop(0, n_tiles, lambda i, c: sweep(1, n_tiles - 1 - i, c), zero)


def _ssm_scan(s, at, batch, lane_block):
    _, r, lanes = s.shape
    assert V7X_SUBLANES % batch == 0 and r % V7X_SUBLANES == 0
    return pl.pallas_call(
        functools.partial(_ssm_scan_body, batch=batch, n_tiles=r // V7X_SUBLANES),
        grid=(lanes // lane_block,),
        in_specs=[pl.BlockSpec((N_SLABS, r, lane_block), lambda k: (0, 0, k)),
                  pl.BlockSpec((N_SLABS, lane_block), lambda k: (0, k))],
        out_specs=pl.BlockSpec((N_SLABS, r, lane_block), lambda k: (0, 0, k)),
        out_shape=jax.ShapeDtypeStruct(s.shape, F32),
        compiler_params=_params(("parallel",)),
        name="ssm_scan",
    )(s, at)


def _ssm_mix_body(x_ref, wt_ref, hin_ref, wo_ref, y_ref):
    x = x_ref[0]
    hin = jnp.concatenate([hin_ref[slab] for slab in range(N_SLABS)], axis=1).astype(BF16)
    carried = jnp.dot(hin, wo_ref[0], preferred_element_type=F32)
    for g2 in range(2):
        cols = slice(g2 * CHUNK_W, (g2 + 1) * CHUNK_W)
        y_ref[0, :, cols] = carried[:, cols] + jnp.dot(x[:, cols], wt_ref[0, g2],
                                                       preferred_element_type=F32)


def _ssm_mix(x, wt, hin, wo):
    np_, r, _ = x.shape
    return pl.pallas_call(
        _ssm_mix_body,
        grid=(np_,),
        in_specs=[pl.BlockSpec((1, r, 2 * CHUNK_W), lambda k: (k, 0, 0)),
                  pl.BlockSpec((1,) + wt.shape[1:], lambda k: (k, 0, 0, 0)),
                  pl.BlockSpec((N_SLABS, r, PAIR_STATE), lambda k: (0, 0, k)),
                  pl.BlockSpec((1,) + wo.shape[1:], lambda k: (k, 0, 0))],
        out_specs=pl.BlockSpec((1, r, 2 * CHUNK_W), lambda k: (k, 0, 0)),
        out_shape=jax.ShapeDtypeStruct((np_, r, 2 * CHUNK_W), F32),
        compiler_params=_params(("parallel",)),
        name="ssm_mix",
    )(x, wt, hin, wo)


def _outproj_body(x_ref, a_ref, y_ref, u_ref, d_ref, wg_ref, bg_ref, gs_ref, wo_ref, gf_ref,
                  x1_ref, h_ref):
    y = jax.nn.gelu(y_ref[...] + d_ref[...] * u_ref[...])
    gate = jnp.dot(y.astype(BF16), wg_ref[...], preferred_element_type=F32) + bg_ref[...]
    s = _rms(y * jax.nn.sigmoid(gate), gs_ref[...], NORM_EPS)
    mixed = (jnp.dot(a_ref[...], wo_ref[:ATTN_WIDTH, :], preferred_element_type=F32)
             + jnp.dot(s.astype(BF16), wo_ref[ATTN_WIDTH:, :], preferred_element_type=F32))
    x1 = x_ref[...] + mixed
    x1_ref[...] = x1
    h_ref[...] = _rms(x1, gf_ref[...], NORM_EPS).astype(h_ref.dtype)


def _outproj(x2, a2, y2, u2, d, w_glu, b_glu, g_ssm, w_out, g_ffn, tm):
    n_tok = x2.shape[0]
    row = lambda w: pl.BlockSpec((tm, w), lambda i: (i, 0))
    full = lambda x: pl.BlockSpec(x.shape, lambda i: (0, 0))
    return pl.pallas_call(
        _outproj_body,
        grid=(n_tok // tm,),
        in_specs=[row(D_MODEL), row(ATTN_WIDTH), row(SSM_WIDTH), row(SSM_WIDTH),
                  full(d), full(w_glu), full(b_glu), full(g_ssm), full(w_out), full(g_ffn)],
        out_specs=[row(D_MODEL), row(D_MODEL)],
        out_shape=[jax.ShapeDtypeStruct((n_tok, D_MODEL), F32),
                   jax.ShapeDtypeStruct((n_tok, D_MODEL), BF16)],
        compiler_params=_params(("parallel",)),
        name="outproj",
    )(x2, a2, y2, u2, d, w_glu, b_glu, g_ssm, w_out, g_ffn)


def _ffn_body(h_ref, hp_ref, hn_ref, x1_ref, wu_ref, cw_ref, cb_ref, wd_ref, gf_ref, o_ref,
              hcat_ref, *, tm, tiles_per_seq, ff_chunk):
    i = pl.program_id(0)
    halo = V7X_BF16_ROWS
    first = (i % tiles_per_seq) == 0
    last = (i % tiles_per_seq) == tiles_per_seq - 1
    zeros = jnp.zeros((halo, D_MODEL), hcat_ref.dtype)
    hcat_ref[:halo, :] = jnp.where(first, zeros, hp_ref[...])
    hcat_ref[halo:halo + tm, :] = h_ref[...]
    hcat_ref[halo + tm:, :] = jnp.where(last, zeros, hn_ref[...])
    hcat = hcat_ref[...]
    rows = tm + 2 * halo

    def conv(z, col0):
        cols = pl.ds(col0, ff_chunk)
        prev = pltpu.roll(z, 1, 0)[halo:halo + tm]
        nxt = pltpu.roll(z, rows - 1, 0)[halo:halo + tm]
        return (prev * cw_ref[pl.ds(0, 1), cols] + z[halo:halo + tm] * cw_ref[pl.ds(1, 1), cols]
                + nxt * cw_ref[pl.ds(2, 1), cols] + cb_ref[:, cols])

    acc = jnp.zeros((tm, D_MODEL), F32)
    for c in range(D_FF // ff_chunk):
        g0 = c * ff_chunk
        v0 = D_FF + c * ff_chunk
        zg = jnp.dot(hcat, wu_ref[:, pl.ds(g0, ff_chunk)], preferred_element_type=F32)
        zv = jnp.dot(hcat, wu_ref[:, pl.ds(v0, ff_chunk)], preferred_element_type=F32)
        act = (jax.nn.gelu(conv(zg, g0)) * conv(zv, v0)).astype(BF16)
        acc = acc + jnp.dot(act, wd_ref[pl.ds(g0, ff_chunk), :], preferred_element_type=F32)
    o_ref[...] = _rms(x1_ref[...] + acc, gf_ref[...], NORM_EPS)


def _ffn(h2, x1, w_up, conv_w, conv_b, w_down, g_final, tm, seq, ff_chunk):
    n_tok = h2.shape[0]
    halo = V7X_BF16_ROWS
    per = tm // halo
    n_halo = n_tok // halo
    row = lambda w: pl.BlockSpec((tm, w), lambda i: (i, 0))
    once = lambda x: pl.BlockSpec(x.shape, lambda i: (0, 0), pipeline_mode=pl.Buffered(1))
    return pl.pallas_call(
        functools.partial(_ffn_body, tm=tm, tiles_per_seq=seq // tm, ff_chunk=ff_chunk),
        grid=(n_tok // tm,),
        in_specs=[row(D_MODEL),
                  pl.BlockSpec((halo, D_MODEL), lambda i: (jnp.maximum(i * per - 1, 0), 0)),
                  pl.BlockSpec((halo, D_MODEL),
                               lambda i: (jnp.minimum((i + 1) * per, n_halo - 1), 0)),
                  row(D_MODEL), once(w_up), once(conv_w), once(conv_b), once(w_down),
                  once(g_final)],
        out_specs=row(D_MODEL),
        out_shape=jax.ShapeDtypeStruct((n_tok, D_MODEL), F32),
        scratch_shapes=[pltpu.VMEM((tm + 2 * halo, D_MODEL), BF16)],
        compiler_params=_params(("parallel",)),
        name="ffn",
    )(h2, h2, h2, x1, w_up, conv_w, conv_b, w_down, g_final)


def _tile(pref, n):
    t = min(pref, n)
    assert n % t == 0, (pref, n)
    return t


def _to_chunks(u, batch, seq):
    nc = seq // SSM_CHUNK
    x = u.astype(BF16).reshape(batch, nc, SSM_CHUNK, N_GROUP_PAIRS, 2, SSM_GROUP)
    return x.transpose(3, 1, 0, 4, 2, 5).reshape(N_GROUP_PAIRS, nc * batch, 2 * CHUNK_W)


def _from_chunks(y, batch, seq):
    nc = seq // SSM_CHUNK
    y = y.reshape(N_GROUP_PAIRS, nc, batch, 2, SSM_CHUNK, SSM_GROUP)
    return y.transpose(2, 1, 4, 0, 3, 5).reshape(batch * seq, SSM_WIDTH)


def _trunk(x, p, ssm_w):
    batch, seq, _ = x.shape
    n_tok = batch * seq
    x2 = x.reshape(n_tok, D_MODEL)
    tm = _tile(512, seq)

    q, k, vt, u = _inproj(x2, p["g_mix"], p["w_qku"], p["w_vt"], tm)

    shape3 = (batch, seq, ATTN_WIDTH)
    a = _attention(q.reshape(shape3), k.reshape(shape3), vt,
                   p["lq1"], p["lk1"], p["lq2"], p["lk2"], p["g_subln"],
                   _tile(512, seq), _tile(512, seq))

    wt, ws, wo, at = ssm_w
    xc = _to_chunks(u, batch, seq)
    s_in = _ssm_inject(xc, ws)
    at_flat = at.transpose(1, 0, 2).reshape(N_SLABS, N_GROUP_PAIRS * PAIR_STATE)
    hin = _ssm_scan(s_in, at_flat, batch, 4 * PAIR_STATE)
    y = _from_chunks(_ssm_mix(xc, wt, hin, wo), batch, seq)

    x1, h = _outproj(x2, a.reshape(n_tok, ATTN_WIDTH), y, u, p["d"], p["w_glu"], p["b_glu"],
                     p["g_ssm"], p["w_out"], p["g_ffn"], tm)
    out = _ffn(h, x1, p["w_up"], p["conv_w"], p["conv_b"], p["w_down"], p["g_final"],
               _tile(1024, seq), seq, 256)
    return out.reshape(batch, seq, D_MODEL)


def kernel(x_prompt, x_sample, g_mix_norm, w_in, lambda_q1, lambda_k1, lambda_q2, lambda_k2, g_subln, ssm_a_re, ssm_a_im, ssm_log_step, ssm_b_re, ssm_b_im, ssm_c_re, ssm_c_im, ssm_d, w_glu, b_glu, g_ssm_out, w_out, g_ffn_norm, w_up, conv_w, conv_b, w_down, g_final):
    layer = 0
    vec = lambda t: t.reshape(1, -1).astype(F32)
    p = {
        "g_mix": vec(g_mix_norm[layer]),
        "w_qku": jnp.concatenate([w_in[layer][:, :2 * ATTN_WIDTH],
                                  w_in[layer][:, 3 * ATTN_WIDTH:]], axis=1).astype(BF16),
        "w_vt": w_in[layer][:, 2 * ATTN_WIDTH:3 * ATTN_WIDTH].T.astype(BF16),
        "lq1": vec(lambda_q1[layer]), "lk1": vec(lambda_k1[layer]),
        "lq2": vec(lambda_q2[layer]), "lk2": vec(lambda_k2[layer]),
        "g_subln": g_subln[layer].reshape(-1, 1).astype(F32), "d": vec(ssm_d[layer]),
        "w_glu": w_glu[layer].astype(BF16), "b_glu": vec(b_glu[layer]),
        "g_ssm": vec(g_ssm_out[layer]), "w_out": w_out[layer].astype(BF16),
        "g_ffn": vec(g_ffn_norm[layer]), "w_up": w_up[layer].astype(BF16),
        "conv_w": conv_w[layer].astype(F32), "conv_b": vec(conv_b[layer]),
        "w_down": w_down[layer].astype(BF16), "g_final": vec(g_final),
    }
    f32 = lambda t: t[layer].astype(F32)
    ssm_w = _ssm_weights(f32(ssm_a_re), f32(ssm_a_im), f32(ssm_log_step), f32(ssm_b_re),
                         f32(ssm_b_im), f32(ssm_c_re), f32(ssm_c_im))
    return _trunk(x_prompt, p, ssm_w), _trunk(x_sample, p, ssm_w)
```

```python
import functools
import math

import jax
import jax.numpy as jnp
import numpy as np
from jax import lax
from jax.experimental import pallas as pl
from jax.experimental.pallas import tpu as pltpu

F32 = jnp.float32
BF16 = jnp.bfloat16

D_MODEL = 1024
ATTN_WIDTH = 512
SSM_WIDTH = 512
N_HEADS = 4
HEAD_DIM = 64
HEAD_WIDTH = 2 * HEAD_DIM
SSM_GROUP = 16
N_SSM_GROUPS = 32
N_GROUP_PAIRS = N_SSM_GROUPS // 2
STATE = 64
PAIR_STATE = 2 * STATE
D_FF = 2816
NORM_EPS = 1e-6
SUBLN_EPS = 1e-5
LAM_INIT = 0.8 - 0.6 * math.exp(-0.3 * 0)
LOG2E = math.log2(math.e)
N_BIAS_LANES = 3
ATTN_PAIRS_PER_TRIP = 2

SSM_CHUNK = 32
CHUNK_W = SSM_CHUNK * SSM_GROUP
LAG_W = 2 * CHUNK_W
POW_BITS = SSM_CHUNK.bit_length()
N_SLABS = 4

V7X_MXU_WIDTH = 256
V7X_SUBLANES = 8
V7X_BF16_ROWS = 16
V7X_VMEM_LIMIT = 56 * 1024 * 1024

HI = lax.Precision.HIGHEST


def _params(sem, vmem=V7X_VMEM_LIMIT, flags=None):
    return pltpu.CompilerParams(dimension_semantics=sem, vmem_limit_bytes=vmem, flags=flags)


def _rms(x, g, eps):
    return x * lax.rsqrt(jnp.mean(x * x, axis=-1, keepdims=True) + eps) * g


RELAYOUT_ROWS = 512
CHUNKS_PER_TILE = RELAYOUT_ROWS // SSM_CHUNK
LANE = 128
STEP_LO = LANE // SSM_GROUP
STEP_HI = SSM_CHUNK // STEP_LO
LANE_TILES = SSM_WIDTH // LANE
PERM_W = STEP_LO * LANE


def _relayout_constants():
    rows = np.zeros((RELAYOUT_ROWS, RELAYOUT_ROWS), np.float32)
    for c in range(CHUNKS_PER_TILE):
        for s in range(SSM_CHUNK):
            rows[s * CHUNKS_PER_TILE + c, c * SSM_CHUNK + s] = 1.0
    lanes = np.zeros((PERM_W, PERM_W), np.float32)
    for s_lo in range(STEP_LO):
        for g8 in range(STEP_LO):
            for i in range(SSM_GROUP):
                lanes[s_lo * LANE + g8 * SSM_GROUP + i, g8 * LANE + s_lo * SSM_GROUP + i] = 1.0
    as_bf = lambda m: jnp.asarray(m, BF16)
    return as_bf(rows), as_bf(lanes), as_bf(lanes.T), as_bf(rows.T)


def _pair_lanes(lane_tile, g8, step_hi):
    group = lane_tile * STEP_LO + g8
    return group // 2, (group % 2) * CHUNK_W + step_hi * LANE


def _to_chunk_layout(u, rowperm_ref, laneperm_ref, x_ref):
    nct = CHUNKS_PER_TILE
    up = jnp.dot(rowperm_ref[...], u.astype(BF16), preferred_element_type=F32).astype(BF16)
    stacked = jnp.concatenate(
        [jnp.concatenate([up[(hi * STEP_LO + lo) * nct:(hi * STEP_LO + lo + 1) * nct,
                             lt * LANE:(lt + 1) * LANE] for lo in range(STEP_LO)], axis=1)
         for lt in range(LANE_TILES) for hi in range(STEP_HI)], axis=0)
    out = jnp.dot(stacked, laneperm_ref[...], preferred_element_type=F32).astype(BF16)
    for lt in range(LANE_TILES):
        for hi in range(STEP_HI):
            r0 = (lt * STEP_HI + hi) * nct
            for g8 in range(STEP_LO):
                pair, lane0 = _pair_lanes(lt, g8, hi)
                x_ref[pair, :, pl.ds(lane0, LANE)] = out[r0:r0 + nct, g8 * LANE:(g8 + 1) * LANE]


def _from_chunk_layout(y_ref, laneperm_t_ref, rowperm_t_ref):
    nct = CHUNKS_PER_TILE
    pieces = []
    for lt in range(LANE_TILES):
        for hi in range(STEP_HI):
            row = []
            for g8 in range(STEP_LO):
                pair, lane0 = _pair_lanes(lt, g8, hi)
                row.append(y_ref[pair, :, pl.ds(lane0, LANE)])
            pieces.append(jnp.concatenate(row, axis=1))
    stacked = jnp.concatenate(pieces, axis=0)
    high = stacked.astype(BF16)
    low = (stacked - high.astype(F32)).astype(BF16)
    z2 = jnp.dot(jnp.concatenate([high, low], axis=0), laneperm_t_ref[...],
                 preferred_element_type=F32)
    z = z2[:LANE_TILES * STEP_HI * nct] + z2[LANE_TILES * STEP_HI * nct:]
    perm = jnp.concatenate(
        [jnp.concatenate([z[(lt * STEP_HI + hi) * nct:(lt * STEP_HI + hi + 1) * nct,
                            lo * LANE:(lo + 1) * LANE] for lt in range(LANE_TILES)], axis=1)
         for hi in range(STEP_HI) for lo in range(STEP_LO)], axis=0)
    high = perm.astype(BF16)
    low = (perm - high.astype(F32)).astype(BF16)
    y2 = jnp.dot(rowperm_t_ref[...], jnp.concatenate([high, low], axis=1),
                 preferred_element_type=F32)
    return y2[:, :SSM_WIDTH] + y2[:, SSM_WIDTH:]


def _inproj_body(x_ref, g_ref, wqku_ref, wvt_ref, rowperm_ref, laneperm_ref,
                 q_ref, k_ref, vt_ref, u_ref, xc_ref):
    n = _rms(x_ref[...], g_ref[...], NORM_EPS).astype(BF16)
    proj = jnp.dot(n, wqku_ref[...], preferred_element_type=F32)
    a = ATTN_WIDTH
    q_ref[...] = (proj[:, :a] * (LOG2E * HEAD_DIM ** -0.5)).astype(BF16)
    k_ref[...] = proj[:, a:2 * a].astype(BF16)
    u = proj[:, 2 * a:]
    u_ref[...] = u
    _to_chunk_layout(u, rowperm_ref, laneperm_ref, xc_ref)
    vt_ref[...] = lax.dot_general(wvt_ref[...], n, (((1,), (1,)), ((), ())),
                                  preferred_element_type=F32).astype(BF16)


def _inproj(x2, g, w_qku, w_vt, rowperm, laneperm):
    n_tok = x2.shape[0]
    tm = RELAYOUT_ROWS
    row = lambda w: pl.BlockSpec((tm, w), lambda i: (i, 0))
    full = lambda x: pl.BlockSpec(x.shape, lambda i: (0, 0))
    chunk_block = pl.BlockSpec((N_GROUP_PAIRS, CHUNKS_PER_TILE, 2 * CHUNK_W), lambda i: (0, i, 0))
    return pl.pallas_call(
        _inproj_body,
        grid=(n_tok // tm,),
        in_specs=[row(D_MODEL), full(g), full(w_qku), full(w_vt), full(rowperm), full(laneperm)],
        out_specs=[row(ATTN_WIDTH), row(ATTN_WIDTH),
                   pl.BlockSpec((ATTN_WIDTH, tm), lambda i: (0, i)), row(SSM_WIDTH), chunk_block],
        out_shape=[jax.ShapeDtypeStruct((n_tok, ATTN_WIDTH), BF16),
                   jax.ShapeDtypeStruct((n_tok, ATTN_WIDTH), BF16),
                   jax.ShapeDtypeStruct((ATTN_WIDTH, n_tok), BF16),
                   jax.ShapeDtypeStruct((n_tok, SSM_WIDTH), F32),
                   jax.ShapeDtypeStruct((N_GROUP_PAIRS, n_tok // SSM_CHUNK, 2 * CHUNK_W), BF16)],
        compiler_params=_params(("parallel",)),
        name="inproj",
    )(x2, g, w_qku, w_vt, rowperm, laneperm)


def _attn_body(q_ref, k_ref, vt_ref, lq1_ref, lk1_ref, lq2_ref, lk2_ref, g_ref, o_ref,
               qa_ref, kaug_ref, s_ref, p_ref, mx_ref, beta_ref, alpha_ref, m_ref, acc_ref,
               *, tq, tk, nk):
    h = pl.program_id(1)
    i = pl.program_id(2)
    tq2 = 2 * tq
    sigma = jnp.float32(0.0)
    for head in range(N_HEADS):
        sigma = jnp.where(h == head,
                          jnp.float32(LOG2E * 2.0 ** (-8.0 * (head + 1) / N_HEADS)), sigma)
    q0 = i * tq
    jd = q0 // tk

    q = q_ref[0]
    lane = lax.broadcasted_iota(jnp.int32, q.shape, 1)
    zero = jnp.zeros_like(q)
    ones3 = jnp.where(lane < N_BIAS_LANES, 1.0, 0.0).astype(BF16)
    qa_ref[:tq, :HEAD_WIDTH] = jnp.where(lane < HEAD_DIM, q, zero)
    qa_ref[tq:, :HEAD_WIDTH] = jnp.where(lane < HEAD_DIM, zero, q)
    qa_ref[:tq, HEAD_WIDTH:] = ones3
    qa_ref[tq:, HEAD_WIDTH:] = ones3

    @pl.when(i == 0)
    def _():
        koff = lax.broadcasted_iota(jnp.int32, (tk, HEAD_WIDTH), 0).astype(F32) * sigma
        klane = lax.broadcasted_iota(jnp.int32, (tk, HEAD_WIDTH), 1)
        hi = koff.astype(BF16).astype(F32)
        mid = (koff - hi).astype(BF16).astype(F32)
        lo = koff - hi - mid
        ktile = jnp.where(klane == 0, hi,
                          jnp.where(klane == 1, mid, jnp.where(klane == 2, lo, 0.0)))
        kaug_ref[0] = ktile.astype(BF16)
        kaug_ref[1] = (-ktile).astype(BF16)

    ql = lax.broadcasted_iota(jnp.int32, (1, tq2), 1)
    qpos = (q0 + jnp.where(ql < tq, ql, ql - tq)).astype(F32)
    ones_rows = jnp.where(
        lax.broadcasted_iota(jnp.int32, (V7X_BF16_ROWS, tk), 0) == 0, 1.0, 0.0).astype(BF16)

    def block_of(n):
        return jnp.where(n == 0, jd, jnp.where(n - 1 < jd, n - 1, n))

    lane_chunks = [pl.ds(c * V7X_MXU_WIDTH, V7X_MXU_WIDTH) for c in range(tq2 // V7X_MXU_WIDTH)]
    kk = lax.broadcasted_iota(jnp.int32, (tk, V7X_MXU_WIDTH), 0)
    qq = lax.broadcasted_iota(jnp.int32, (tk, V7X_MXU_WIDTH), 1)

    def keys_of(n):
        j = block_of(n)
        right = (j > jd).astype(jnp.int32)
        start = pl.multiple_of(j * tk, tk)
        ka = jnp.concatenate([k_ref[0, pl.ds(start, tk), :], kaug_ref[right]], axis=1)
        sgn = jnp.where(j > jd, jnp.float32(-1.0), jnp.float32(1.0))
        beta = (((j * tk).astype(F32) - qpos) * sigma) * sgn
        return ka, beta

    def scores_chunk(ka, beta, slot, c, diagonal=False):
        cs = lane_chunks[c]
        st = lax.dot_general(ka, qa_ref[cs, :], (((1,), (1,)), ((), ())),
                             preferred_element_type=F32)
        if diagonal:
            qoff = (c * V7X_MXU_WIDTH) % tq
            st = st + (jnp.maximum(kk - qq + (jd * tk - q0 - qoff), 0).astype(F32)
                       * sigma) * -2.0
        s_ref[slot, :, cs] = st
        mx_ref[slot, :, cs] = jnp.max(st, axis=0, keepdims=True) + beta[:, c * V7X_MXU_WIDTH:
                                                                      (c + 1) * V7X_MXU_WIDTH]

    def softmax_chunk(slot, c):
        cs = lane_chunks[c]
        m_prev = m_ref[:, cs]
        m_new = jnp.maximum(m_prev, mx_ref[slot, :, cs])
        alpha_ref[slot, :, cs] = jnp.exp2(m_prev - m_new)
        p_ref[slot, :, cs] = jnp.exp2(s_ref[slot, :, cs]
                                      - (m_new - beta_ref[slot, :, cs])).astype(BF16)
        m_ref[:, cs] = m_new

    def pv_chunk(vt, slot, c):
        cs = lane_chunks[c]
        acc_ref[:, cs] = alpha_ref[slot, :, cs] * acc_ref[:, cs] + jnp.dot(
            vt, p_ref[slot, :, cs], preferred_element_type=F32)

    def values_of(n):
        start = pl.multiple_of(block_of(n) * tk, tk)
        return jnp.concatenate([vt_ref[:, pl.ds(start, tk)], ones_rows], axis=0)

    def pipeline_step(n_scores, n_softmax, n_pv, diagonal=False):
        if n_scores is not None:
            ka, beta = keys_of(n_scores[0])
            beta_ref[n_scores[1]] = beta
        if n_pv is not None:
            vt = values_of(n_pv[0])
        for c in range(len(lane_chunks)):
            if n_softmax is not None:
                softmax_chunk(n_softmax, c)
            if n_scores is not None:
                scores_chunk(ka, beta, n_scores[1], c, diagonal)
            if n_pv is not None:
                pv_chunk(vt, n_pv[1], c)

    m_ref[...] = jnp.full(m_ref.shape, -jnp.inf, F32)
    acc_ref[...] = jnp.zeros(acc_ref.shape, F32)
    pipeline_step((0, 0), None, None, diagonal=True)
    pipeline_step((1, 1), 0, None)

    def steady(n, pairs):
        for k in range(pairs):
            pipeline_step((n + 2 * k + 2, 0), 1, (n + 2 * k, 0))
            pipeline_step((n + 2 * k + 3, 1), 0, (n + 2 * k + 1, 1))

    n_pairs = nk // 2 - 1
    trips = n_pairs // ATTN_PAIRS_PER_TRIP

    def trip(t, carry):
        steady(2 * ATTN_PAIRS_PER_TRIP * t, ATTN_PAIRS_PER_TRIP)
        return carry

    lax.fori_loop(0, trips, trip, 0)
    if n_pairs % ATTN_PAIRS_PER_TRIP:
        steady(2 * ATTN_PAIRS_PER_TRIP * trips, n_pairs % ATTN_PAIRS_PER_TRIP)
    pipeline_step(None, 1, (nk - 2, 0))
    pipeline_step(None, None, (nk - 1, 1))

    lam = (jnp.exp(jnp.sum(lq1_ref[...] * lk1_ref[...], axis=-1, keepdims=True))
           - jnp.exp(jnp.sum(lq2_ref[...] * lk2_ref[...], axis=-1, keepdims=True))
           + LAM_INIT)
    acc = acc_ref[...]
    o = acc[:HEAD_WIDTH, :] * (1.0 / acc[HEAD_WIDTH:HEAD_WIDTH + 1, :])
    at = o[:, :tq] - lam * o[:, tq:]
    ms = jnp.mean(at * at, axis=0, keepdims=True)
    at = at * lax.rsqrt(ms + SUBLN_EPS) * g_ref[...] * (1.0 - LAM_INIT)
    o_ref[0] = at.T.astype(o_ref.dtype)


def _attention(q, k, vt, lq1, lk1, lq2, lk2, g_subln_col, tq, tk):
    b, s, _ = q.shape
    nk = s // tk
    assert tk % tq == 0 and nk % 2 == 0 and nk >= 2
    vec = lambda n: pl.BlockSpec((1, n), lambda bi, hi, qi: (0, 0))
    return pl.pallas_call(
        functools.partial(_attn_body, tq=tq, tk=tk, nk=nk),
        grid=(b, N_HEADS, s // tq),
        in_specs=[pl.BlockSpec((1, tq, HEAD_WIDTH), lambda bi, hi, qi: (bi, qi, hi)),
                  pl.BlockSpec((1, s, HEAD_WIDTH), lambda bi, hi, qi: (bi, 0, hi)),
                  pl.BlockSpec((HEAD_WIDTH, s), lambda bi, hi, qi: (hi, bi)),
                  vec(HEAD_DIM), vec(HEAD_DIM), vec(HEAD_DIM), vec(HEAD_DIM),
                  pl.BlockSpec((HEAD_WIDTH, 1), lambda bi, hi, qi: (0, 0))],
        out_specs=pl.BlockSpec((1, tq, HEAD_WIDTH), lambda bi, hi, qi: (bi, qi, hi)),
        out_shape=jax.ShapeDtypeStruct((b, s, ATTN_WIDTH), BF16),
        scratch_shapes=[pltpu.VMEM((2 * tq, 2 * HEAD_WIDTH), BF16),
                        pltpu.VMEM((2, tk, HEAD_WIDTH), BF16),
                        pltpu.VMEM((2, tk, 2 * tq), F32),
                        pltpu.VMEM((2, tk, 2 * tq), BF16),
                        pltpu.VMEM((2, 1, 2 * tq), F32),
                        pltpu.VMEM((2, 1, 2 * tq), F32),
                        pltpu.VMEM((2, 1, 2 * tq), F32),
                        pltpu.VMEM((1, 2 * tq), F32),
                        pltpu.VMEM((HEAD_WIDTH + V7X_BF16_ROWS, 2 * tq), F32)],
        compiler_params=_params(("parallel", "parallel", "arbitrary")),
        name="diff_attention",
    )(q, k, vt, lq1, lk1, lq2, lk2, g_subln_col)


def _cpow(lre, lim, n):
    shape = jnp.broadcast_shapes(lre.shape, n.shape)
    pre = jnp.ones(shape, F32)
    pim = jnp.zeros(shape, F32)
    bre, bim = lre, lim
    for bit in range(POW_BITS):
        on = ((n >> bit) & 1) == 1
        mre = jnp.where(on, bre, 1.0)
        mim = jnp.where(on, bim, 0.0)
        pre, pim = pre * mre - pim * mim, pre * mim + pim * mre
        bre, bim = bre * bre - bim * bim, 2.0 * bre * bim
    return pre, pim


def _discretise(a_re, a_im, log_step):
    step = jnp.exp(log_step)
    mag = jnp.exp(a_re * step)
    lre = mag * jnp.cos(a_im * step)
    lim = mag * jnp.sin(a_im * step)
    den = a_re * a_re + a_im * a_im
    nr = lre - 1.0
    fre = (nr * a_re + lim * a_im) / den
    fim = (lim * a_re - nr * a_im) / den
    return lre, lim, fre, fim


def _ssm_weights_body(arow_re, arow_im, acol_re, acol_im, lsrow, lscol, bt_re, bt_im,
                      ct_re, ct_im, wt_ref, ws_ref, wo_ref, at_ref):
    t, cw = SSM_CHUNK, CHUNK_W
    lane_pair = lax.broadcasted_iota(jnp.int32, (1, PAIR_STATE), 1) // STATE
    row_pair = lax.broadcasted_iota(jnp.int32, (PAIR_STATE, 1), 0) // STATE

    ch = lax.broadcasted_iota(jnp.int32, (SSM_GROUP, LAG_W), 0)
    ln = lax.broadcasted_iota(jnp.int32, (SSM_GROUP, LAG_W), 1)
    tile_lag = jnp.where((ln % SSM_GROUP) == ch, 1.0, 0.0).astype(F32)

    step_rows = lax.broadcasted_iota(jnp.int32, (cw, 1), 0) // SSM_GROUP
    slot = lax.broadcasted_iota(jnp.int32, (1, LAG_W), 1) // SSM_GROUP
    step_lanes = lax.broadcasted_iota(jnp.int32, (1, cw), 1) // SSM_GROUP

    zt = [jnp.zeros((SSM_GROUP, LAG_W), F32) for _ in range(2)]
    for d in range(2):
        lre, lim, fre, fim = _discretise(arow_re[d, 0], arow_im[d, 0], lsrow[d, 0])
        bre, bim = bt_re[d, 0], bt_im[d, 0]
        bbre = fre * bre - fim * bim
        bbim = fre * bim + fim * bre
        expo = (t - 1 - step_rows) if d == 0 else step_rows
        pre, pim = _cpow(lre, lim, expo)
        tbre = jnp.tile(bbre, (t, 1))
        tbim = jnp.tile(bbim, (t, 1))
        inj_re = pre * tbre - pim * tbim
        inj_im = pre * tbim + pim * tbre
        for g2 in range(2):
            keep = lane_pair == g2
            rows = pl.ds(g2 * cw, cw)
            ws_ref[0, rows, pl.ds((2 * d) * PAIR_STATE, PAIR_STATE)] = (
                jnp.where(keep, inj_re, 0.0).astype(ws_ref.dtype))
            ws_ref[0, rows, pl.ds((2 * d + 1) * PAIR_STATE, PAIR_STATE)] = (
                jnp.where(keep, inj_im, 0.0).astype(ws_ref.dtype))
        dre, dim_ = _cpow(lre, lim, jnp.full((1, 1), t, jnp.int32))
        at_ref[0, pl.ds(2 * d, 1), :] = dre
        at_ref[0, pl.ds(2 * d + 1, 1), :] = dim_

        cre, cim, _, _ = _discretise(acol_re[d, 0], acol_im[d, 0], lscol[d, 0])
        ctl_re = jnp.dot(ct_re[d, 0], tile_lag, precision=HI, preferred_element_type=F32)
        ctl_im = jnp.dot(ct_im[d, 0], tile_lag, precision=HI, preferred_element_type=F32)
        lag = (slot - (t - 1)) if d == 0 else ((t - 1) - slot)
        valid = (lag >= 0) & (slot < 2 * t - 1)
        qre, qim = _cpow(cre, cim, jnp.maximum(lag, 0))
        r_re = jnp.where(valid, qre * ctl_re - qim * ctl_im, 0.0)
        r_im = jnp.where(valid, qre * ctl_im + qim * ctl_re, 0.0)
        for g2 in range(2):
            keep = lane_pair == g2
            zt[g2] = (zt[g2]
                      + jnp.dot(jnp.where(keep, bbre, 0.0), r_re, precision=HI,
                                preferred_element_type=F32)
                      - jnp.dot(jnp.where(keep, bbim, 0.0), r_im, precision=HI,
                                preferred_element_type=F32))

        out_pow = (step_lanes + 1) if d == 0 else (t - step_lanes)
        ore, oim = _cpow(cre, cim, out_pow)
        c_re = ctl_re[:, :cw]
        c_im = ctl_im[:, :cw]
        rd_re = ore * c_re - oim * c_im
        rd_im = -(ore * c_im + oim * c_re)
        for g2 in range(2):
            keep = row_pair == g2
            cols = pl.ds(g2 * cw, cw)
            wo_ref[0, pl.ds((2 * d) * PAIR_STATE, PAIR_STATE), cols] = (
                jnp.where(keep, rd_re, 0.0).astype(wo_ref.dtype))
            wo_ref[0, pl.ds((2 * d + 1) * PAIR_STATE, PAIR_STATE), cols] = (
                jnp.where(keep, rd_im, 0.0).astype(wo_ref.dtype))

    for g2 in range(2):
        for s in range(t):
            off = (t - 1 - s) * SSM_GROUP
            shifted = zt[g2] if off == 0 else pltpu.roll(zt[g2], LAG_W - off, 1)
            wt_ref[0, g2, pl.ds(s * SSM_GROUP, SSM_GROUP), :] = (
                shifted[:, :cw].astype(wt_ref.dtype))


def _ssm_weights(a_re, a_im, log_step, b_re, b_im, c_re, c_im):
    np_, g2p = N_GROUP_PAIRS, PAIR_STATE
    arow = lambda a: a.reshape(2, np_, 1, g2p)
    acol = lambda a: a.reshape(2, np_, g2p, 1)
    ls = jnp.repeat(log_step, STATE, axis=-1)
    bt = lambda w: (w.reshape(2, np_, 2, STATE, SSM_GROUP)
                    .transpose(0, 1, 4, 2, 3).reshape(2, np_, SSM_GROUP, g2p))
    ct = lambda w: (w.reshape(2, np_, 2, SSM_GROUP, STATE)
                    .transpose(0, 1, 2, 4, 3).reshape(2, np_, g2p, SSM_GROUP))
    ins = [arow(a_re), arow(a_im), acol(a_re), acol(a_im), arow(ls), acol(ls),
           bt(b_re), bt(b_im), ct(c_re), ct(c_im)]
    spec = lambda x: pl.BlockSpec((2, 1) + x.shape[2:], lambda k: (0, k, 0, 0))
    return pl.pallas_call(
        _ssm_weights_body,
        grid=(np_,),
        in_specs=[spec(x) for x in ins],
        out_specs=[pl.BlockSpec((1, 2, CHUNK_W, CHUNK_W), lambda k: (k, 0, 0, 0)),
                   pl.BlockSpec((1, 2 * CHUNK_W, N_SLABS * g2p), lambda k: (k, 0, 0)),
                   pl.BlockSpec((1, N_SLABS * g2p, 2 * CHUNK_W), lambda k: (k, 0, 0)),
                   pl.BlockSpec((1, N_SLABS, g2p), lambda k: (k, 0, 0))],
        out_shape=[jax.ShapeDtypeStruct((np_, 2, CHUNK_W, CHUNK_W), BF16),
                   jax.ShapeDtypeStruct((np_, 2 * CHUNK_W, N_SLABS * g2p), BF16),
                   jax.ShapeDtypeStruct((np_, N_SLABS * g2p, 2 * CHUNK_W), BF16),
                   jax.ShapeDtypeStruct((np_, N_SLABS, g2p), F32)],
        compiler_params=_params(("parallel",)),
        name="ssm_weights",
    )(*ins)


def _ssm_inject_body(x_ref, ws_ref, s_ref):
    s = jnp.dot(x_ref[0], ws_ref[0], preferred_element_type=F32)
    for slab in range(N_SLABS):
        s_ref[slab] = s[:, slab * PAIR_STATE:(slab + 1) * PAIR_STATE]


def _ssm_inject(x, ws):
    np_, r, _ = x.shape
    return pl.pallas_call(
        _ssm_inject_body,
        grid=(np_,),
        in_specs=[pl.BlockSpec((1, r, 2 * CHUNK_W), lambda k: (k, 0, 0)),
                  pl.BlockSpec((1,) + ws.shape[1:], lambda k: (k, 0, 0))],
        out_specs=pl.BlockSpec((N_SLABS, r, PAIR_STATE), lambda k: (0, 0, k)),
        out_shape=jax.ShapeDtypeStruct((N_SLABS, r, np_ * PAIR_STATE), F32),
        compiler_params=_params(("parallel",)),
        name="ssm_inject",
    )(x, ws)


def _ssm_scan_body(s_ref, at_ref, hin_ref, *, tiles_per_seq, n_tiles):
    sub = V7X_SUBLANES
    lanes = s_ref.shape[-1]
    a = [at_ref[pl.ds(slab, 1), :] for slab in range(N_SLABS)]

    def sweep(d, tile, carry):
        edge = 0 if d == 0 else tiles_per_seq - 1
        keep = jnp.where(lax.rem(tile, tiles_per_seq) == edge,
                         jnp.float32(0.0), jnp.float32(1.0))
        hre, him = carry[0] * keep, carry[1] * keep
        are, aim = a[2 * d], a[2 * d + 1]
        row0 = pl.multiple_of(tile * V7X_SUBLANES, V7X_SUBLANES)
        sre = s_ref[2 * d, pl.ds(row0, V7X_SUBLANES), :]
        sim = s_ref[2 * d + 1, pl.ds(row0, V7X_SUBLANES), :]
        out_re = [None] * sub
        out_im = [None] * sub
        order = range(sub) if d == 0 else range(sub - 1, -1, -1)
        for q in order:
            out_re[q], out_im[q] = hre, him
            xr = sre[q:q + 1]
            xi = sim[q:q + 1]
            hre, him = are * hre - aim * him + xr, are * him + aim * hre + xi
        hin_ref[2 * d, pl.ds(row0, V7X_SUBLANES), :] = jnp.concatenate(out_re, axis=0)
        hin_ref[2 * d + 1, pl.ds(row0, V7X_SUBLANES), :] = jnp.concatenate(out_im, axis=0)
        return hre, him

    zero = (jnp.zeros((1, lanes), F32), jnp.zeros((1, lanes), F32))
    lax.fori_loop(0, n_tiles, lambda i, c: sweep(0, i, c), zero)
    lax.fori_loop(0, n_tiles, lambda i, c: sweep(1, n_tiles - 1 - i, c), zero)


def _ssm_scan(s, at, chunks_per_seq, lane_block):
    _, r, lanes = s.shape
    assert chunks_per_seq % V7X_SUBLANES == 0 and r % chunks_per_seq == 0
    return pl.pallas_call(
        functools.partial(_ssm_scan_body, tiles_per_seq=chunks_per_seq // V7X_SUBLANES,
                          n_tiles=r // V7X_SUBLANES),
        grid=(lanes // lane_block,),
        in_specs=[pl.BlockSpec((N_SLABS, r, lane_block), lambda k: (0, 0, k)),
                  pl.BlockSpec((N_SLABS, lane_block), lambda k: (0, k))],
        out_specs=pl.BlockSpec((N_SLABS, r, lane_block), lambda k: (0, 0, k)),
        out_shape=jax.ShapeDtypeStruct(s.shape, F32),
        compiler_params=_params(("parallel",)),
        name="ssm_scan",
    )(s, at)


def _ssm_mix_body(x_ref, wt_ref, hin_ref, wo_ref, y_ref):
    x = x_ref[0]
    hin = jnp.concatenate([hin_ref[slab] for slab in range(N_SLABS)], axis=1).astype(BF16)
    carried = jnp.dot(hin, wo_ref[0], preferred_element_type=F32)
    for g2 in range(2):
        cols = slice(g2 * CHUNK_W, (g2 + 1) * CHUNK_W)
        y_ref[0, :, cols] = carried[:, cols] + jnp.dot(x[:, cols], wt_ref[0, g2],
                                                       preferred_element_type=F32)


def _ssm_mix(x, wt, hin, wo):
    np_, r, _ = x.shape
    return pl.pallas_call(
        _ssm_mix_body,
        grid=(np_,),
        in_specs=[pl.BlockSpec((1, r, 2 * CHUNK_W), lambda k: (k, 0, 0)),
                  pl.BlockSpec((1,) + wt.shape[1:], lambda k: (k, 0, 0, 0)),
                  pl.BlockSpec((N_SLABS, r, PAIR_STATE), lambda k: (0, 0, k)),
                  pl.BlockSpec((1,) + wo.shape[1:], lambda k: (k, 0, 0))],
        out_specs=pl.BlockSpec((1, r, 2 * CHUNK_W), lambda k: (k, 0, 0)),
        out_shape=jax.ShapeDtypeStruct((np_, r, 2 * CHUNK_W), F32),
        compiler_params=_params(("parallel",)),
        name="ssm_mix",
    )(x, wt, hin, wo)


def _outproj_body(x_ref, a_ref, yc_ref, u_ref, d_ref, wg_ref, bg_ref, gs_ref, wo_ref, gf_ref,
                  laneperm_t_ref, rowperm_t_ref, x1_ref, h_ref):
    y_ssm = _from_chunk_layout(yc_ref, laneperm_t_ref, rowperm_t_ref)
    y = jax.nn.gelu(y_ssm + d_ref[...] * u_ref[...])
    gate = jnp.dot(y.astype(BF16), wg_ref[...], preferred_element_type=F32) + bg_ref[...]
    s = _rms(y * jax.nn.sigmoid(gate), gs_ref[...], NORM_EPS)
    mixed = (jnp.dot(a_ref[...], wo_ref[:ATTN_WIDTH, :], preferred_element_type=F32)
             + jnp.dot(s.astype(BF16), wo_ref[ATTN_WIDTH:, :], preferred_element_type=F32))
    x1 = x_ref[...] + mixed
    x1_ref[...] = x1
    h_ref[...] = _rms(x1, gf_ref[...], NORM_EPS).astype(h_ref.dtype)


def _outproj(x2, a2, yc, u2, d, w_glu, b_glu, g_ssm, w_out, g_ffn, laneperm_t, rowperm_t):
    n_tok = x2.shape[0]
    tm = RELAYOUT_ROWS
    row = lambda w: pl.BlockSpec((tm, w), lambda i: (i, 0))
    full = lambda x: pl.BlockSpec(x.shape, lambda i: (0, 0))
    chunk_block = pl.BlockSpec((N_GROUP_PAIRS, CHUNKS_PER_TILE, 2 * CHUNK_W), lambda i: (0, i, 0))
    return pl.pallas_call(
        _outproj_body,
        grid=(n_tok // tm,),
        in_specs=[row(D_MODEL), row(ATTN_WIDTH), chunk_block, row(SSM_WIDTH),
                  full(d), full(w_glu), full(b_glu), full(g_ssm), full(w_out), full(g_ffn),
                  full(laneperm_t), full(rowperm_t)],
        out_specs=[row(D_MODEL), row(D_MODEL)],
        out_shape=[jax.ShapeDtypeStruct((n_tok, D_MODEL), F32),
                   jax.ShapeDtypeStruct((n_tok, D_MODEL), BF16)],
        compiler_params=_params(("parallel",)),
        name="outproj",
    )(x2, a2, yc, u2, d, w_glu, b_glu, g_ssm, w_out, g_ffn, laneperm_t, rowperm_t)


def _ffn_body(h_ref, hp_ref, hn_ref, x1_ref, wu_ref, cw_ref, cb_ref, wd_ref, gf_ref, o_ref,
              hcat_ref, *, tm, tiles_per_seq, ff_chunk):
    i = pl.program_id(0)
    halo = V7X_BF16_ROWS
    first = (i % tiles_per_seq) == 0
    last = (i % tiles_per_seq) == tiles_per_seq - 1
    zeros = jnp.zeros((halo, D_MODEL), hcat_ref.dtype)
    hcat_ref[:halo, :] = jnp.where(first, zeros, hp_ref[...])
    hcat_ref[halo:halo + tm, :] = h_ref[...]
    hcat_ref[halo + tm:, :] = jnp.where(last, zeros, hn_ref[...])
    hcat = hcat_ref[...]
    rows = tm + 2 * halo

    def conv(z, col0):
        cols = pl.ds(col0, ff_chunk)
        prev = pltpu.roll(z, 1, 0)[halo:halo + tm]
        nxt = pltpu.roll(z, rows - 1, 0)[halo:halo + tm]
        return (prev * cw_ref[pl.ds(0, 1), cols] + z[halo:halo + tm] * cw_ref[pl.ds(1, 1), cols]
                + nxt * cw_ref[pl.ds(2, 1), cols] + cb_ref[:, cols])

    acc = jnp.zeros((tm, D_MODEL), F32)
    for c in range(D_FF // ff_chunk):
        g0 = c * ff_chunk
        v0 = D_FF + c * ff_chunk
        zg = jnp.dot(hcat, wu_ref[:, pl.ds(g0, ff_chunk)], preferred_element_type=F32)
        zv = jnp.dot(hcat, wu_ref[:, pl.ds(v0, ff_chunk)], preferred_element_type=F32)
        act = (jax.nn.gelu(conv(zg, g0)) * conv(zv, v0)).astype(BF16)
        acc = acc + jnp.dot(act, wd_ref[pl.ds(g0, ff_chunk), :], preferred_element_type=F32)
    o_ref[...] = _rms(x1_ref[...] + acc, gf_ref[...], NORM_EPS)


def _ffn(h2, x1, w_up, conv_w, conv_b, w_down, g_final, tm, seq, ff_chunk):
    n_tok = h2.shape[0]
    halo = V7X_BF16_ROWS
    per = tm // halo
    n_halo = n_tok // halo
    row = lambda w: pl.BlockSpec((tm, w), lambda i: (i, 0))
    once = lambda x: pl.BlockSpec(x.shape, lambda i: (0, 0), pipeline_mode=pl.Buffered(1))
    return pl.pallas_call(
        functools.partial(_ffn_body, tm=tm, tiles_per_seq=seq // tm, ff_chunk=ff_chunk),
        grid=(n_tok // tm,),
        in_specs=[row(D_MODEL),
                  pl.BlockSpec((halo, D_MODEL), lambda i: (jnp.maximum(i * per - 1, 0), 0)),
                  pl.BlockSpec((halo, D_MODEL),
                               lambda i: (jnp.minimum((i + 1) * per, n_halo - 1), 0)),
                  row(D_MODEL), once(w_up), once(conv_w), once(conv_b), once(w_down),
                  once(g_final)],
        out_specs=row(D_MODEL),
        out_shape=jax.ShapeDtypeStruct((n_tok, D_MODEL), F32),
        scratch_shapes=[pltpu.VMEM((tm + 2 * halo, D_MODEL), BF16)],
        compiler_params=_params(("parallel",)),
        name="ffn",
    )(h2, h2, h2, x1, w_up, conv_w, conv_b, w_down, g_final)


def _tile(pref, n):
    t = min(pref, n)
    assert n % t == 0, (pref, n)
    return t


def _trunk(x, p, ssm_w, perms):
    batch, seq, _ = x.shape
    n_tok = batch * seq
    x2 = x.reshape(n_tok, D_MODEL)
    assert seq % RELAYOUT_ROWS == 0
    rowperm, laneperm, laneperm_t, rowperm_t = perms

    q, k, vt, u, xc = _inproj(x2, p["g_mix"], p["w_qku"], p["w_vt"], rowperm, laneperm)

    shape3 = (batch, seq, ATTN_WIDTH)
    a = _attention(q.reshape(shape3), k.reshape(shape3), vt,
                   p["lq1"], p["lk1"], p["lq2"], p["lk2"], p["g_subln"],
                   _tile(512, seq), _tile(512, seq))

    wt, ws, wo, at = ssm_w
    s_in = _ssm_inject(xc, ws)
    at_flat = at.transpose(1, 0, 2).reshape(N_SLABS, N_GROUP_PAIRS * PAIR_STATE)
    hin = _ssm_scan(s_in, at_flat, seq // SSM_CHUNK, 4 * PAIR_STATE)
    yc = _ssm_mix(xc, wt, hin, wo)

    x1, h = _outproj(x2, a.reshape(n_tok, ATTN_WIDTH), yc, u, p["d"], p["w_glu"], p["b_glu"],
                     p["g_ssm"], p["w_out"], p["g_ffn"], laneperm_t, rowperm_t)
    out = _ffn(h, x1, p["w_up"], p["conv_w"], p["conv_b"], p["w_down"], p["g_final"],
               _tile(1024, seq), seq, 256)
    return out.reshape(batch, seq, D_MODEL)


def kernel(x_prompt, x_sample, g_mix_norm, w_in, lambda_q1, lambda_k1, lambda_q2, lambda_k2, g_subln, ssm_a_re, ssm_a_im, ssm_log_step, ssm_b_re, ssm_b_im, ssm_c_re, ssm_c_im, ssm_d, w_glu, b_glu, g_ssm_out, w_out, g_ffn_norm, w_up, conv_w, conv_b, w_down, g_final):
    layer = 0
    vec = lambda t: t.reshape(1, -1).astype(F32)
    p = {
        "g_mix": vec(g_mix_norm[layer]),
        "w_qku": jnp.concatenate([w_in[layer][:, :2 * ATTN_WIDTH],
                                  w_in[layer][:, 3 * ATTN_WIDTH:]], axis=1).astype(BF16),
        "w_vt": w_in[layer][:, 2 * ATTN_WIDTH:3 * ATTN_WIDTH].T.astype(BF16),
        "lq1": vec(lambda_q1[layer]), "lk1": vec(lambda_k1[layer]),
        "lq2": vec(lambda_q2[layer]), "lk2": vec(lambda_k2[layer]),
        "g_subln": g_subln[layer].reshape(-1, 1).astype(F32), "d": vec(ssm_d[layer]),
        "w_glu": w_glu[layer].astype(BF16), "b_glu": vec(b_glu[layer]),
        "g_ssm": vec(g_ssm_out[layer]), "w_out": w_out[layer].astype(BF16),
        "g_ffn": vec(g_ffn_norm[layer]), "w_up": w_up[layer].astype(BF16),
        "conv_w": conv_w[layer].astype(F32), "conv_b": vec(conv_b[layer]),
        "w_down": w_down[layer].astype(BF16), "g_final": vec(g_final),
    }
    f32 = lambda t: t[layer].astype(F32)
    ssm_w = _ssm_weights(f32(ssm_a_re), f32(ssm_a_im), f32(ssm_log_step), f32(ssm_b_re),
                         f32(ssm_b_im), f32(ssm_c_re), f32(ssm_c_im))
    perms = _relayout_constants()
    return _trunk(x_prompt, p, ssm_w, perms), _trunk(x_sample, p, ssm_w, perms)
```

```python
import functools
import math

import jax
import jax.numpy as jnp
import numpy as np
from jax import lax
from jax.experimental import pallas as pl
from jax.experimental.pallas import tpu as pltpu

F32 = jnp.float32
BF16 = jnp.bfloat16

D_MODEL = 1024
ATTN_WIDTH = 512
SSM_WIDTH = 512
N_HEADS = 4
HEAD_DIM = 64
HEAD_WIDTH = 2 * HEAD_DIM
SSM_GROUP = 16
N_SSM_GROUPS = 32
N_GROUP_PAIRS = N_SSM_GROUPS // 2
STATE = 64
PAIR_STATE = 2 * STATE
D_FF = 2816
NORM_EPS = 1e-6
SUBLN_EPS = 1e-5
LAM_INIT = 0.8 - 0.6 * math.exp(-0.3 * 0)
LOG2E = math.log2(math.e)
N_BIAS_LANES = 3
ATTN_UNDERFLOW_LOG2 = 152.0
ATTN_NORM_SLACK = 1.001
ATTN_WINDOW_MIN_BLOCKS = 8

SSM_CHUNK = 32
CHUNK_W = SSM_CHUNK * SSM_GROUP
LAG_W = 2 * CHUNK_W
POW_BITS = SSM_CHUNK.bit_length()
N_SLABS = 4

V7X_MXU_WIDTH = 256
V7X_SUBLANES = 8
V7X_BF16_ROWS = 16
V7X_VMEM_LIMIT = 56 * 1024 * 1024

HI = lax.Precision.HIGHEST


def _params(sem, vmem=V7X_VMEM_LIMIT, flags=None):
    return pltpu.CompilerParams(dimension_semantics=sem, vmem_limit_bytes=vmem, flags=flags)


def _rms(x, g, eps):
    return x * lax.rsqrt(jnp.mean(x * x, axis=-1, keepdims=True) + eps) * g


RELAYOUT_ROWS = 512
CHUNKS_PER_TILE = RELAYOUT_ROWS // SSM_CHUNK
LANE = 128
STEP_LO = LANE // SSM_GROUP
STEP_HI = SSM_CHUNK // STEP_LO
LANE_TILES = SSM_WIDTH // LANE
PERM_W = STEP_LO * LANE


def _relayout_constants():
    rows = np.zeros((RELAYOUT_ROWS, RELAYOUT_ROWS), np.float32)
    for c in range(CHUNKS_PER_TILE):
        for s in range(SSM_CHUNK):
            rows[s * CHUNKS_PER_TILE + c, c * SSM_CHUNK + s] = 1.0
    lanes = np.zeros((PERM_W, PERM_W), np.float32)
    for s_lo in range(STEP_LO):
        for g8 in range(STEP_LO):
            for i in range(SSM_GROUP):
                lanes[s_lo * LANE + g8 * SSM_GROUP + i, g8 * LANE + s_lo * SSM_GROUP + i] = 1.0
    as_bf = lambda m: jnp.asarray(m, BF16)
    return as_bf(rows), as_bf(lanes), as_bf(lanes.T), as_bf(rows.T)


def _pair_lanes(lane_tile, g8, step_hi):
    group = lane_tile * STEP_LO + g8
    return group // 2, (group % 2) * CHUNK_W + step_hi * LANE


def _to_chunk_layout(u, rowperm_ref, laneperm_ref, x_ref):
    nct = CHUNKS_PER_TILE
    up = jnp.dot(rowperm_ref[...], u.astype(BF16), preferred_element_type=F32).astype(BF16)
    stacked = jnp.concatenate(
        [jnp.concatenate([up[(hi * STEP_LO + lo) * nct:(hi * STEP_LO + lo + 1) * nct,
                             lt * LANE:(lt + 1) * LANE] for lo in range(STEP_LO)], axis=1)
         for lt in range(LANE_TILES) for hi in range(STEP_HI)], axis=0)
    out = jnp.dot(stacked, laneperm_ref[...], preferred_element_type=F32).astype(BF16)
    for lt in range(LANE_TILES):
        for hi in range(STEP_HI):
            r0 = (lt * STEP_HI + hi) * nct
            for g8 in range(STEP_LO):
                pair, lane0 = _pair_lanes(lt, g8, hi)
                x_ref[pair, :, pl.ds(lane0, LANE)] = out[r0:r0 + nct, g8 * LANE:(g8 + 1) * LANE]


def _from_chunk_layout(y_ref, laneperm_t_ref, rowperm_t_ref):
    nct = CHUNKS_PER_TILE
    pieces = []
    for lt in range(LANE_TILES):
        for hi in range(STEP_HI):
            row = []
            for g8 in range(STEP_LO):
                pair, lane0 = _pair_lanes(lt, g8, hi)
                row.append(y_ref[pair, :, pl.ds(lane0, LANE)])
            pieces.append(jnp.concatenate(row, axis=1))
    stacked = jnp.concatenate(pieces, axis=0)
    high = stacked.astype(BF16)
    low = (stacked - high.astype(F32)).astype(BF16)
    z2 = jnp.dot(jnp.concatenate([high, low], axis=0), laneperm_t_ref[...],
                 preferred_element_type=F32)
    z = z2[:LANE_TILES * STEP_HI * nct] + z2[LANE_TILES * STEP_HI * nct:]
    perm = jnp.concatenate(
        [jnp.concatenate([z[(lt * STEP_HI + hi) * nct:(lt * STEP_HI + hi + 1) * nct,
                            lo * LANE:(lo + 1) * LANE] for lt in range(LANE_TILES)], axis=1)
         for hi in range(STEP_HI) for lo in range(STEP_LO)], axis=0)
    high = perm.astype(BF16)
    low = (perm - high.astype(F32)).astype(BF16)
    y2 = jnp.dot(rowperm_t_ref[...], jnp.concatenate([high, low], axis=1),
                 preferred_element_type=F32)
    return y2[:, :SSM_WIDTH] + y2[:, SSM_WIDTH:]


def _inproj_body(x_ref, g_ref, wqku_ref, wvt_ref, rowperm_ref, laneperm_ref,
                 q_ref, k_ref, vt_ref, u_ref, xc_ref):
    n = _rms(x_ref[...], g_ref[...], NORM_EPS).astype(BF16)
    proj = jnp.dot(n, wqku_ref[...], preferred_element_type=F32)
    a = ATTN_WIDTH
    q_ref[...] = (proj[:, :a] * (LOG2E * HEAD_DIM ** -0.5)).astype(BF16)
    k_ref[...] = proj[:, a:2 * a].astype(BF16)
    u = proj[:, 2 * a:]
    u_ref[...] = u
    _to_chunk_layout(u, rowperm_ref, laneperm_ref, xc_ref)
    vt_ref[...] = lax.dot_general(wvt_ref[...], n, (((1,), (1,)), ((), ())),
                                  preferred_element_type=F32).astype(BF16)


def _inproj(x2, g, w_qku, w_vt, rowperm, laneperm):
    n_tok = x2.shape[0]
    tm = RELAYOUT_ROWS
    row = lambda w: pl.BlockSpec((tm, w), lambda i: (i, 0))
    full = lambda x: pl.BlockSpec(x.shape, lambda i: (0, 0))
    chunk_block = pl.BlockSpec((N_GROUP_PAIRS, CHUNKS_PER_TILE, 2 * CHUNK_W), lambda i: (0, i, 0))
    return pl.pallas_call(
        _inproj_body,
        grid=(n_tok // tm,),
        in_specs=[row(D_MODEL), full(g), full(w_qku), full(w_vt), full(rowperm), full(laneperm)],
        out_specs=[row(ATTN_WIDTH), row(ATTN_WIDTH),
                   pl.BlockSpec((ATTN_WIDTH, tm), lambda i: (0, i)), row(SSM_WIDTH), chunk_block],
        out_shape=[jax.ShapeDtypeStruct((n_tok, ATTN_WIDTH), BF16),
                   jax.ShapeDtypeStruct((n_tok, ATTN_WIDTH), BF16),
                   jax.ShapeDtypeStruct((ATTN_WIDTH, n_tok), BF16),
                   jax.ShapeDtypeStruct((n_tok, SSM_WIDTH), F32),
                   jax.ShapeDtypeStruct((N_GROUP_PAIRS, n_tok // SSM_CHUNK, 2 * CHUNK_W), BF16)],
        compiler_params=_params(("parallel",)),
        name="inproj",
    )(x2, g, w_qku, w_vt, rowperm, laneperm)


def _max_half_norm_sq(x):
    sq = x.astype(F32)
    sq = sq * sq
    lane = lax.broadcasted_iota(jnp.int32, sq.shape, 1)
    first = jnp.sum(jnp.where(lane < HEAD_DIM, sq, 0.0), axis=1, keepdims=True)
    second = jnp.sum(jnp.where(lane < HEAD_DIM, 0.0, sq), axis=1, keepdims=True)
    return jnp.max(jnp.maximum(first, second), axis=0, keepdims=True)


def _attn_body(q_ref, k_ref, vt_ref, lq1_ref, lk1_ref, lq2_ref, lk2_ref, g_ref, o_ref,
               qa_ref, kaug_ref, s_ref, p_ref, mx_ref, beta_ref, alpha_ref, m_ref, acc_ref,
               knorm_ref, *, tq, tk, nk, windowed):
    h = pl.program_id(1)
    i = pl.program_id(2)
    tq2 = 2 * tq
    sigma = jnp.float32(0.0)
    for head in range(N_HEADS):
        sigma = jnp.where(h == head,
                          jnp.float32(LOG2E * 2.0 ** (-8.0 * (head + 1) / N_HEADS)), sigma)
    q0 = i * tq
    jd = q0 // tk

    q = q_ref[0]
    lane = lax.broadcasted_iota(jnp.int32, q.shape, 1)
    zero = jnp.zeros_like(q)
    ones3 = jnp.where(lane < N_BIAS_LANES, 1.0, 0.0).astype(BF16)
    qa_ref[:tq, :HEAD_WIDTH] = jnp.where(lane < HEAD_DIM, q, zero)
    qa_ref[tq:, :HEAD_WIDTH] = jnp.where(lane < HEAD_DIM, zero, q)
    qa_ref[:tq, HEAD_WIDTH:] = ones3
    qa_ref[tq:, HEAD_WIDTH:] = ones3

    @pl.when(i == 0)
    def _():
        koff = lax.broadcasted_iota(jnp.int32, (tk, HEAD_WIDTH), 0).astype(F32) * sigma
        klane = lax.broadcasted_iota(jnp.int32, (tk, HEAD_WIDTH), 1)
        hi = koff.astype(BF16).astype(F32)
        mid = (koff - hi).astype(BF16).astype(F32)
        lo = koff - hi - mid
        ktile = jnp.where(klane == 0, hi,
                          jnp.where(klane == 1, mid, jnp.where(klane == 2, lo, 0.0)))
        kaug_ref[0] = ktile.astype(BF16)
        kaug_ref[1] = (-ktile).astype(BF16)

        if windowed:
            def knorm(c, best):
                rows = k_ref[0, pl.ds(pl.multiple_of(c * tk, tk), tk), :]
                return jnp.maximum(best, _max_half_norm_sq(rows))

            knorm_ref[...] = lax.fori_loop(0, nk, knorm, jnp.zeros((1, 1), F32))

    if windowed:
        qk = jnp.sqrt(_max_half_norm_sq(q) * knorm_ref[...]) * ATTN_NORM_SLACK
        reach = (2.0 * qk + ATTN_UNDERFLOW_LOG2) / sigma
        radius = jnp.maximum(jnp.floor((reach - 1.0) / tk) + 1.0, 0.0)
        radius = jnp.minimum(radius, float(nk)).astype(jnp.int32)[0, 0]
        jlo = jnp.maximum(jd - radius, 0)
        jhi = jnp.minimum(jd + radius, nk - 1)
        odd = lax.rem(jhi - jlo + 1, 2) == 1
        grow_hi = jnp.logical_and(odd, jhi < nk - 1)
        grow_lo = jnp.logical_and(odd, jhi >= nk - 1)
        jhi = jnp.where(grow_hi, jhi + 1, jhi)
        jlo = jnp.where(grow_lo, jlo - 1, jlo)
        nb = jhi - jlo + 1
    else:
        jlo, nb = 0, nk

    ql = lax.broadcasted_iota(jnp.int32, (1, tq2), 1)
    qpos = (q0 + jnp.where(ql < tq, ql, ql - tq)).astype(F32)
    ones_rows = jnp.where(
        lax.broadcasted_iota(jnp.int32, (V7X_BF16_ROWS, tk), 0) == 0, 1.0, 0.0).astype(BF16)

    def block_of(n):
        j = jlo + n - 1
        return jnp.where(n == 0, jd, jnp.where(j < jd, j, j + 1))

    lane_chunks = [pl.ds(c * V7X_MXU_WIDTH, V7X_MXU_WIDTH) for c in range(tq2 // V7X_MXU_WIDTH)]
    kk = lax.broadcasted_iota(jnp.int32, (tk, V7X_MXU_WIDTH), 0)
    qq = lax.broadcasted_iota(jnp.int32, (tk, V7X_MXU_WIDTH), 1)

    def keys_of(n):
        j = block_of(n)
        right = (j > jd).astype(jnp.int32)
        start = pl.multiple_of(j * tk, tk)
        ka = jnp.concatenate([k_ref[0, pl.ds(start, tk), :], kaug_ref[right]], axis=1)
        sgn = jnp.where(j > jd, jnp.float32(-1.0), jnp.float32(1.0))
        beta = (((j * tk).astype(F32) - qpos) * sigma) * sgn
        return ka, beta

    def scores_chunk(ka, beta, slot, c, diagonal=False):
        cs = lane_chunks[c]
        st = lax.dot_general(ka, qa_ref[cs, :], (((1,), (1,)), ((), ())),
                             preferred_element_type=F32)
        if diagonal:
            qoff = (c * V7X_MXU_WIDTH) % tq
            st = st + (jnp.maximum(kk - qq + (jd * tk - q0 - qoff), 0).astype(F32)
                       * sigma) * -2.0
        s_ref[slot, :, cs] = st
        mx_ref[slot, :, cs] = jnp.max(st, axis=0, keepdims=True) + beta[:, c * V7X_MXU_WIDTH:
                                                                      (c + 1) * V7X_MXU_WIDTH]

    def softmax_chunk(slot, c):
        cs = lane_chunks[c]
        m_prev = m_ref[:, cs]
        m_new = jnp.maximum(m_prev, mx_ref[slot, :, cs])
        alpha_ref[slot, :, cs] = jnp.exp2(m_prev - m_new)
        p_ref[slot, :, cs] = jnp.exp2(s_ref[slot, :, cs]
                                      - (m_new - beta_ref[slot, :, cs])).astype(BF16)
        m_ref[:, cs] = m_new

    def pv_chunk(vt, slot, c):
        cs = lane_chunks[c]
        acc_ref[:, cs] = alpha_ref[slot, :, cs] * acc_ref[:, cs] + jnp.dot(
            vt, p_ref[slot, :, cs], preferred_element_type=F32)

    def values_of(n):
        start = pl.multiple_of(block_of(n) * tk, tk)
        return jnp.concatenate([vt_ref[:, pl.ds(start, tk)], ones_rows], axis=0)

    def pipeline_step(n_scores, n_softmax, n_pv, diagonal=False):
        if n_scores is not None:
            ka, beta = keys_of(n_scores[0])
            beta_ref[n_scores[1]] = beta
        if n_pv is not None:
            vt = values_of(n_pv[0])
        for c in range(len(lane_chunks)):
            if n_softmax is not None:
                softmax_chunk(n_softmax, c)
            if n_scores is not None:
                scores_chunk(ka, beta, n_scores[1], c, diagonal)
            if n_pv is not None:
                pv_chunk(vt, n_pv[1], c)

    m_ref[...] = jnp.full(m_ref.shape, -jnp.inf, F32)
    acc_ref[...] = jnp.zeros(acc_ref.shape, F32)
    pipeline_step((0, 0), None, None, diagonal=True)
    pipeline_step((1, 1), 0, None)

    def pair(t, carry):
        n = 2 * t
        pipeline_step((n + 2, 0), 1, (n, 0))
        pipeline_step((n + 3, 1), 0, (n + 1, 1))
        return carry

    if windowed:
        lax.fori_loop(0, lax.div(nb, 2) - 1, pair, 0)
    else:
        n_pairs = nk // 2 - 1
        lax.fori_loop(0, n_pairs // 2, lambda t, c: pair(2 * t + 1, pair(2 * t, c)), 0)
        if n_pairs % 2:
            pair(n_pairs - 1, 0)
    pipeline_step(None, 1, (nb - 2, 0))
    pipeline_step(None, None, (nb - 1, 1))

    lam = (jnp.exp(jnp.sum(lq1_ref[...] * lk1_ref[...], axis=-1, keepdims=True))
           - jnp.exp(jnp.sum(lq2_ref[...] * lk2_ref[...], axis=-1, keepdims=True))
           + LAM_INIT)
    acc = acc_ref[...]
    o = acc[:HEAD_WIDTH, :] * (1.0 / acc[HEAD_WIDTH:HEAD_WIDTH + 1, :])
    at = o[:, :tq] - lam * o[:, tq:]
    ms = jnp.mean(at * at, axis=0, keepdims=True)
    at = at * lax.rsqrt(ms + SUBLN_EPS) * g_ref[...] * (1.0 - LAM_INIT)
    o_ref[0] = at.T.astype(o_ref.dtype)


def _attention(q, k, vt, lq1, lk1, lq2, lk2, g_subln_col, tq, tk):
    b, s, _ = q.shape
    nk = s // tk
    assert tk % tq == 0 and nk % 2 == 0 and nk >= 2
    vec = lambda n: pl.BlockSpec((1, n), lambda bi, hi, qi: (0, 0))
    return pl.pallas_call(
        functools.partial(_attn_body, tq=tq, tk=tk, nk=nk,
                          windowed=nk > ATTN_WINDOW_MIN_BLOCKS),
        grid=(b, N_HEADS, s // tq),
        in_specs=[pl.BlockSpec((1, tq, HEAD_WIDTH), lambda bi, hi, qi: (bi, qi, hi)),
                  pl.BlockSpec((1, s, HEAD_WIDTH), lambda bi, hi, qi: (bi, 0, hi)),
                  pl.BlockSpec((HEAD_WIDTH, s), lambda bi, hi, qi: (hi, bi)),
                  vec(HEAD_DIM), vec(HEAD_DIM), vec(HEAD_DIM), vec(HEAD_DIM),
                  pl.BlockSpec((HEAD_WIDTH, 1), lambda bi, hi, qi: (0, 0))],
        out_specs=pl.BlockSpec((1, tq, HEAD_WIDTH), lambda bi, hi, qi: (bi, qi, hi)),
        out_shape=jax.ShapeDtypeStruct((b, s, ATTN_WIDTH), BF16),
        scratch_shapes=[pltpu.VMEM((2 * tq, 2 * HEAD_WIDTH), BF16),
                        pltpu.VMEM((2, tk, HEAD_WIDTH), BF16),
                        pltpu.VMEM((2, tk, 2 * tq), F32),
                        pltpu.VMEM((2, tk, 2 * tq), BF16),
                        pltpu.VMEM((2, 1, 2 * tq), F32),
                        pltpu.VMEM((2, 1, 2 * tq), F32),
                        pltpu.VMEM((2, 1, 2 * tq), F32),
                        pltpu.VMEM((1, 2 * tq), F32),
                        pltpu.VMEM((HEAD_WIDTH + V7X_BF16_ROWS, 2 * tq), F32),
                        pltpu.VMEM((1, 1), F32)],
        compiler_params=_params(("parallel", "parallel", "arbitrary")),
        name="diff_attention",
    )(q, k, vt, lq1, lk1, lq2, lk2, g_subln_col)


def _cpow(lre, lim, n):
    shape = jnp.broadcast_shapes(lre.shape, n.shape)
    pre = jnp.ones(shape, F32)
    pim = jnp.zeros(shape, F32)
    bre, bim = lre, lim
    for bit in range(POW_BITS):
        on = ((n >> bit) & 1) == 1
        mre = jnp.where(on, bre, 1.0)
        mim = jnp.where(on, bim, 0.0)
        pre, pim = pre * mre - pim * mim, pre * mim + pim * mre
        bre, bim = bre * bre - bim * bim, 2.0 * bre * bim
    return pre, pim


def _discretise(a_re, a_im, log_step):
    step = jnp.exp(log_step)
    mag = jnp.exp(a_re * step)
    lre = mag * jnp.cos(a_im * step)
    lim = mag * jnp.sin(a_im * step)
    den = a_re * a_re + a_im * a_im
    nr = lre - 1.0
    fre = (nr * a_re + lim * a_im) / den
    fim = (lim * a_re - nr * a_im) / den
    return lre, lim, fre, fim


def _ssm_weights_body(arow_re, arow_im, acol_re, acol_im, lsrow, lscol, bt_re, bt_im,
                      ct_re, ct_im, wt_ref, ws_ref, wo_ref, at_ref):
    t, cw = SSM_CHUNK, CHUNK_W
    lane_pair = lax.broadcasted_iota(jnp.int32, (1, PAIR_STATE), 1) // STATE
    row_pair = lax.broadcasted_iota(jnp.int32, (PAIR_STATE, 1), 0) // STATE

    ch = lax.broadcasted_iota(jnp.int32, (SSM_GROUP, LAG_W), 0)
    ln = lax.broadcasted_iota(jnp.int32, (SSM_GROUP, LAG_W), 1)
    tile_lag = jnp.where((ln % SSM_GROUP) == ch, 1.0, 0.0).astype(F32)

    step_rows = lax.broadcasted_iota(jnp.int32, (cw, 1), 0) // SSM_GROUP
    slot = lax.broadcasted_iota(jnp.int32, (1, LAG_W), 1) // SSM_GROUP
    step_lanes = lax.broadcasted_iota(jnp.int32, (1, cw), 1) // SSM_GROUP

    zt = [jnp.zeros((SSM_GROUP, LAG_W), F32) for _ in range(2)]
    for d in range(2):
        lre, lim, fre, fim = _discretise(arow_re[d, 0], arow_im[d, 0], lsrow[d, 0])
        bre, bim = bt_re[d, 0], bt_im[d, 0]
        bbre = fre * bre - fim * bim
        bbim = fre * bim + fim * bre
        expo = (t - 1 - step_rows) if d == 0 else step_rows
        pre, pim = _cpow(lre, lim, expo)
        tbre = jnp.tile(bbre, (t, 1))
        tbim = jnp.tile(bbim, (t, 1))
        inj_re = pre * tbre - pim * tbim
        inj_im = pre * tbim + pim * tbre
        for g2 in range(2):
            keep = lane_pair == g2
            rows = pl.ds(g2 * cw, cw)
            ws_ref[0, rows, pl.ds((2 * d) * PAIR_STATE, PAIR_STATE)] = (
                jnp.where(keep, inj_re, 0.0).astype(ws_ref.dtype))
            ws_ref[0, rows, pl.ds((2 * d + 1) * PAIR_STATE, PAIR_STATE)] = (
                jnp.where(keep, inj_im, 0.0).astype(ws_ref.dtype))
        dre, dim_ = _cpow(lre, lim, jnp.full((1, 1), t, jnp.int32))
        at_ref[0, pl.ds(2 * d, 1), :] = dre
        at_ref[0, pl.ds(2 * d + 1, 1), :] = dim_

        cre, cim, _, _ = _discretise(acol_re[d, 0], acol_im[d, 0], lscol[d, 0])
        ctl_re = jnp.dot(ct_re[d, 0], tile_lag, precision=HI, preferred_element_type=F32)
        ctl_im = jnp.dot(ct_im[d, 0], tile_lag, precision=HI, preferred_element_type=F32)
        lag = (slot - (t - 1)) if d == 0 else ((t - 1) - slot)
        valid = (lag >= 0) & (slot < 2 * t - 1)
        qre, qim = _cpow(cre, cim, jnp.maximum(lag, 0))
        r_re = jnp.where(valid, qre * ctl_re - qim * ctl_im, 0.0)
        r_im = jnp.where(valid, qre * ctl_im + qim * ctl_re, 0.0)
        for g2 in range(2):
            keep = lane_pair == g2
            zt[g2] = (zt[g2]
                      + jnp.dot(jnp.where(keep, bbre, 0.0), r_re, precision=HI,
                                preferred_element_type=F32)
                      - jnp.dot(jnp.where(keep, bbim, 0.0), r_im, precision=HI,
                                preferred_element_type=F32))

        out_pow = (step_lanes + 1) if d == 0 else (t - step_lanes)
        ore, oim = _cpow(cre, cim, out_pow)
        c_re = ctl_re[:, :cw]
        c_im = ctl_im[:, :cw]
        rd_re = ore * c_re - oim * c_im
        rd_im = -(ore * c_im + oim * c_re)
        for g2 in range(2):
            keep = row_pair == g2
            cols = pl.ds(g2 * cw, cw)
            wo_ref[0, pl.ds((2 * d) * PAIR_STATE, PAIR_STATE), cols] = (
                jnp.where(keep, rd_re, 0.0).astype(wo_ref.dtype))
            wo_ref[0, pl.ds((2 * d + 1) * PAIR_STATE, PAIR_STATE), cols] = (
                jnp.where(keep, rd_im, 0.0).astype(wo_ref.dtype))

    for g2 in range(2):
        for s in range(t):
            off = (t - 1 - s) * SSM_GROUP
            shifted = zt[g2] if off == 0 else pltpu.roll(zt[g2], LAG_W - off, 1)
            wt_ref[0, g2, pl.ds(s * SSM_GROUP, SSM_GROUP), :] = (
                shifted[:, :cw].astype(wt_ref.dtype))


def _ssm_weights(a_re, a_im, log_step, b_re, b_im, c_re, c_im):
    np_, g2p = N_GROUP_PAIRS, PAIR_STATE
    arow = lambda a: a.reshape(2, np_, 1, g2p)
    acol = lambda a: a.reshape(2, np_, g2p, 1)
    ls = jnp.repeat(log_step, STATE, axis=-1)
    bt = lambda w: (w.reshape(2, np_, 2, STATE, SSM_GROUP)
                    .transpose(0, 1, 4, 2, 3).reshape(2, np_, SSM_GROUP, g2p))
    ct = lambda w: (w.reshape(2, np_, 2, SSM_GROUP, STATE)
                    .transpose(0, 1, 2, 4, 3).reshape(2, np_, g2p, SSM_GROUP))
    ins = [arow(a_re), arow(a_im), acol(a_re), acol(a_im), arow(ls), acol(ls),
           bt(b_re), bt(b_im), ct(c_re), ct(c_im)]
    spec = lambda x: pl.BlockSpec((2, 1) + x.shape[2:], lambda k: (0, k, 0, 0))
    return pl.pallas_call(
        _ssm_weights_body,
        grid=(np_,),
        in_specs=[spec(x) for x in ins],
        out_specs=[pl.BlockSpec((1, 2, CHUNK_W, CHUNK_W), lambda k: (k, 0, 0, 0)),
                   pl.BlockSpec((1, 2 * CHUNK_W, N_SLABS * g2p), lambda k: (k, 0, 0)),
                   pl.BlockSpec((1, N_SLABS * g2p, 2 * CHUNK_W), lambda k: (k, 0, 0)),
                   pl.BlockSpec((1, N_SLABS, g2p), lambda k: (k, 0, 0))],
        out_shape=[jax.ShapeDtypeStruct((np_, 2, CHUNK_W, CHUNK_W), BF16),
                   jax.ShapeDtypeStruct((np_, 2 * CHUNK_W, N_SLABS * g2p), BF16),
                   jax.ShapeDtypeStruct((np_, N_SLABS * g2p, 2 * CHUNK_W), BF16),
                   jax.ShapeDtypeStruct((np_, N_SLABS, g2p), F32)],
        compiler_params=_params(("parallel",)),
        name="ssm_weights",
    )(*ins)


def _ssm_inject_body(x_ref, ws_ref, s_ref):
    s = jnp.dot(x_ref[0], ws_ref[0], preferred_element_type=F32)
    for slab in range(N_SLABS):
        s_ref[slab] = s[:, slab * PAIR_STATE:(slab + 1) * PAIR_STATE]


def _ssm_inject(x, ws):
    np_, r, _ = x.shape
    return pl.pallas_call(
        _ssm_inject_body,
        grid=(np_,),
        in_specs=[pl.BlockSpec((1, r, 2 * CHUNK_W), lambda k: (k, 0, 0)),
                  pl.BlockSpec((1,) + ws.shape[1:], lambda k: (k, 0, 0))],
        out_specs=pl.BlockSpec((N_SLABS, r, PAIR_STATE), lambda k: (0, 0, k)),
        out_shape=jax.ShapeDtypeStruct((N_SLABS, r, np_ * PAIR_STATE), F32),
        compiler_params=_params(("parallel",)),
        name="ssm_inject",
    )(x, ws)


def _ssm_scan_body(s_ref, at_ref, hin_ref, *, tiles_per_seq, n_tiles):
    sub = V7X_SUBLANES
    lanes = s_ref.shape[-1]
    a = [at_ref[pl.ds(slab, 1), :] for slab in range(N_SLABS)]

    def sweep(d, tile, carry):
        edge = 0 if d == 0 else tiles_per_seq - 1
        keep = jnp.where(lax.rem(tile, tiles_per_seq) == edge,
                         jnp.float32(0.0), jnp.float32(1.0))
        hre, him = carry[0] * keep, carry[1] * keep
        are, aim = a[2 * d], a[2 * d + 1]
        row0 = pl.multiple_of(tile * V7X_SUBLANES, V7X_SUBLANES)
        sre = s_ref[2 * d, pl.ds(row0, V7X_SUBLANES), :]
        sim = s_ref[2 * d + 1, pl.ds(row0, V7X_SUBLANES), :]
        out_re = [None] * sub
        out_im = [None] * sub
        order = range(sub) if d == 0 else range(sub - 1, -1, -1)
        for q in order:
            out_re[q], out_im[q] = hre, him
            xr = sre[q:q + 1]
            xi = sim[q:q + 1]
            hre, him = are * hre - aim * him + xr, are * him + aim * hre + xi
        hin_ref[2 * d, pl.ds(row0, V7X_SUBLANES), :] = jnp.concatenate(out_re, axis=0)
        hin_ref[2 * d + 1, pl.ds(row0, V7X_SUBLANES), :] = jnp.concatenate(out_im, axis=0)
        return hre, him

    zero = (jnp.zeros((1, lanes), F32), jnp.zeros((1, lanes), F32))
    lax.fori_loop(0, n_tiles, lambda i, c: sweep(0, i, c), zero)
    lax.fori_loop(0, n_tiles, lambda i, c: sweep(1, n_tiles - 1 - i, c), zero)


def _ssm_scan(s, at, chunks_per_seq, lane_block):
    _, r, lanes = s.shape
    assert chunks_per_seq % V7X_SUBLANES == 0 and r % chunks_per_seq == 0
    return pl.pallas_call(
        functools.partial(_ssm_scan_body, tiles_per_seq=chunks_per_seq // V7X_SUBLANES,
                          n_tiles=r // V7X_SUBLANES),
        grid=(lanes // lane_block,),
        in_specs=[pl.BlockSpec((N_SLABS, r, lane_block), lambda k: (0, 0, k)),
                  pl.BlockSpec((N_SLABS, lane_block), lambda k: (0, k))],
        out_specs=pl.BlockSpec((N_SLABS, r, lane_block), lambda k: (0, 0, k)),
        out_shape=jax.ShapeDtypeStruct(s.shape, F32),
        compiler_params=_params(("parallel",)),
        name="ssm_scan",
    )(s, at)


def _ssm_mix_body(x_ref, wt_ref, hin_ref, wo_ref, y_ref):
    x = x_ref[0]
    hin = jnp.concatenate([hin_ref[slab] for slab in range(N_SLABS)], axis=1).astype(BF16)
    carried = jnp.dot(hin, wo_ref[0], preferred_element_type=F32)
    for g2 in range(2):
        cols = slice(g2 * CHUNK_W, (g2 + 1) * CHUNK_W)
        y_ref[0, :, cols] = carried[:, cols] + jnp.dot(x[:, cols], wt_ref[0, g2],
                                                       preferred_element_type=F32)


def _ssm_mix(x, wt, hin, wo):
    np_, r, _ = x.shape
    return pl.pallas_call(
        _ssm_mix_body,
        grid=(np_,),
        in_specs=[pl.BlockSpec((1, r, 2 * CHUNK_W), lambda k: (k, 0, 0)),
                  pl.BlockSpec((1,) + wt.shape[1:], lambda k: (k, 0, 0, 0)),
                  pl.BlockSpec((N_SLABS, r, PAIR_STATE), lambda k: (0, 0, k)),
                  pl.BlockSpec((1,) + wo.shape[1:], lambda k: (k, 0, 0))],
        out_specs=pl.BlockSpec((1, r, 2 * CHUNK_W), lambda k: (k, 0, 0)),
        out_shape=jax.ShapeDtypeStruct((np_, r, 2 * CHUNK_W), F32),
        compiler_params=_params(("parallel",)),
        name="ssm_mix",
    )(x, wt, hin, wo)


def _outproj_body(x_ref, a_ref, yc_ref, u_ref, d_ref, wg_ref, bg_ref, gs_ref, wo_ref, gf_ref,
                  laneperm_t_ref, rowperm_t_ref, x1_ref, h_ref):
    y_ssm = _from_chunk_layout(yc_ref, laneperm_t_ref, rowperm_t_ref)
    y = jax.nn.gelu(y_ssm + d_ref[...] * u_ref[...])
    gate = jnp.dot(y.astype(BF16), wg_ref[...], preferred_element_type=F32) + bg_ref[...]
    s = _rms(y * jax.nn.sigmoid(gate), gs_ref[...], NORM_EPS)
    mixed = (jnp.dot(a_ref[...], wo_ref[:ATTN_WIDTH, :], preferred_element_type=F32)
             + jnp.dot(s.astype(BF16), wo_ref[ATTN_WIDTH:, :], preferred_element_type=F32))
    x1 = x_ref[...] + mixed
    x1_ref[...] = x1
    h_ref[...] = _rms(x1, gf_ref[...], NORM_EPS).astype(h_ref.dtype)


def _outproj(x2, a2, yc, u2, d, w_glu, b_glu, g_ssm, w_out, g_ffn, laneperm_t, rowperm_t):
    n_tok = x2.shape[0]
    tm = RELAYOUT_ROWS
    row = lambda w: pl.BlockSpec((tm, w), lambda i: (i, 0))
    full = lambda x: pl.BlockSpec(x.shape, lambda i: (0, 0))
    chunk_block = pl.BlockSpec((N_GROUP_PAIRS, CHUNKS_PER_TILE, 2 * CHUNK_W), lambda i: (0, i, 0))
    return pl.pallas_call(
        _outproj_body,
        grid=(n_tok // tm,),
        in_specs=[row(D_MODEL), row(ATTN_WIDTH), chunk_block, row(SSM_WIDTH),
                  full(d), full(w_glu), full(b_glu), full(g_ssm), full(w_out), full(g_ffn),
                  full(laneperm_t), full(rowperm_t)],
        out_specs=[row(D_MODEL), row(D_MODEL)],
        out_shape=[jax.ShapeDtypeStruct((n_tok, D_MODEL), F32),
                   jax.ShapeDtypeStruct((n_tok, D_MODEL), BF16)],
        compiler_params=_params(("parallel",)),
        name="outproj",
    )(x2, a2, yc, u2, d, w_glu, b_glu, g_ssm, w_out, g_ffn, laneperm_t, rowperm_t)


def _ffn_body(h_ref, hp_ref, hn_ref, x1_ref, wu_ref, cw_ref, cb_ref, wd_ref, gf_ref, o_ref,
              hcat_ref, *, tm, tiles_per_seq, ff_chunk):
    i = pl.program_id(0)
    halo = V7X_BF16_ROWS
    first = (i % tiles_per_seq) == 0
    last = (i % tiles_per_seq) == tiles_per_seq - 1
    zeros = jnp.zeros((halo, D_MODEL), hcat_ref.dtype)
    hcat_ref[:halo, :] = jnp.where(first, zeros, hp_ref[...])
    hcat_ref[halo:halo + tm, :] = h_ref[...]
    hcat_ref[halo + tm:, :] = jnp.where(last, zeros, hn_ref[...])
    hcat = hcat_ref[...]
    rows = tm + 2 * halo

    def conv(z, col0):
        cols = pl.ds(col0, ff_chunk)
        prev = pltpu.roll(z, 1, 0)[halo:halo + tm]
        nxt = pltpu.roll(z, rows - 1, 0)[halo:halo + tm]
        return (prev * cw_ref[pl.ds(0, 1), cols] + z[halo:halo + tm] * cw_ref[pl.ds(1, 1), cols]
                + nxt * cw_ref[pl.ds(2, 1), cols] + cb_ref[:, cols])

    acc = jnp.zeros((tm, D_MODEL), F32)
    for c in range(D_FF // ff_chunk):
        g0 = c * ff_chunk
        v0 = D_FF + c * ff_chunk
        zg = jnp.dot(hcat, wu_ref[:, pl.ds(g0, ff_chunk)], preferred_element_type=F32)
        zv = jnp.dot(hcat, wu_ref[:, pl.ds(v0, ff_chunk)], preferred_element_type=F32)
        act = (jax.nn.gelu(conv(zg, g0)) * conv(zv, v0)).astype(BF16)
        acc = acc + jnp.dot(act, wd_ref[pl.ds(g0, ff_chunk), :], preferred_element_type=F32)
    o_ref[...] = _rms(x1_ref[...] + acc, gf_ref[...], NORM_EPS)


def _ffn(h2, x1, w_up, conv_w, conv_b, w_down, g_final, tm, seq, ff_chunk):
    n_tok = h2.shape[0]
    halo = V7X_BF16_ROWS
    per = tm // halo
    n_halo = n_tok // halo
    row = lambda w: pl.BlockSpec((tm, w), lambda i: (i, 0))
    once = lambda x: pl.BlockSpec(x.shape, lambda i: (0, 0), pipeline_mode=pl.Buffered(1))
    return pl.pallas_call(
        functools.partial(_ffn_body, tm=tm, tiles_per_seq=seq // tm, ff_chunk=ff_chunk),
        grid=(n_tok // tm,),
        in_specs=[row(D_MODEL),
                  pl.BlockSpec((halo, D_MODEL), lambda i: (jnp.maximum(i * per - 1, 0), 0)),
                  pl.BlockSpec((halo, D_MODEL),
                               lambda i: (jnp.minimum((i + 1) * per, n_halo - 1), 0)),
                  row(D_MODEL), once(w_up), once(conv_w), once(conv_b), once(w_down),
                  once(g_final)],
        out_specs=row(D_MODEL),
        out_shape=jax.ShapeDtypeStruct((n_tok, D_MODEL), F32),
        scratch_shapes=[pltpu.VMEM((tm + 2 * halo, D_MODEL), BF16)],
        compiler_params=_params(("parallel",)),
        name="ffn",
    )(h2, h2, h2, x1, w_up, conv_w, conv_b, w_down, g_final)


def _tile(pref, n):
    t = min(pref, n)
    assert n % t == 0, (pref, n)
    return t


def _trunk(x, p, ssm_w, perms):
    batch, seq, _ = x.shape
    n_tok = batch * seq
    x2 = x.reshape(n_tok, D_MODEL)
    assert seq % RELAYOUT_ROWS == 0
    rowperm, laneperm, laneperm_t, rowperm_t = perms

    q, k, vt, u, xc = _inproj(x2, p["g_mix"], p["w_qku"], p["w_vt"], rowperm, laneperm)

    shape3 = (batch, seq, ATTN_WIDTH)
    a = _attention(q.reshape(shape3), k.reshape(shape3), vt,
                   p["lq1"], p["lk1"], p["lq2"], p["lk2"], p["g_subln"],
                   _tile(512, seq), _tile(512, seq))

    wt, ws, wo, at = ssm_w
    s_in = _ssm_inject(xc, ws)
    at_flat = at.transpose(1, 0, 2).reshape(N_SLABS, N_GROUP_PAIRS * PAIR_STATE)
    hin = _ssm_scan(s_in, at_flat, seq // SSM_CHUNK, 4 * PAIR_STATE)
    yc = _ssm_mix(xc, wt, hin, wo)

    x1, h = _outproj(x2, a.reshape(n_tok, ATTN_WIDTH), yc, u, p["d"], p["w_glu"], p["b_glu"],
                     p["g_ssm"], p["w_out"], p["g_ffn"], laneperm_t, rowperm_t)
    out = _ffn(h, x1, p["w_up"], p["conv_w"], p["conv_b"], p["w_down"], p["g_final"],
               _tile(1024, seq), seq, 256)
    return out.reshape(batch, seq, D_MODEL)


def kernel(x_prompt, x_sample, g_mix_norm, w_in, lambda_q1, lambda_k1, lambda_q2, lambda_k2, g_subln, ssm_a_re, ssm_a_im, ssm_log_step, ssm_b_re, ssm_b_im, ssm_c_re, ssm_c_im, ssm_d, w_glu, b_glu, g_ssm_out, w_out, g_ffn_norm, w_up, conv_w, conv_b, w_down, g_final):
    layer = 0
    vec = lambda t: t.reshape(1, -1).astype(F32)
    p = {
        "g_mix": vec(g_mix_norm[layer]),
        "w_qku": jnp.concatenate([w_in[layer][:, :2 * ATTN_WIDTH],
                                  w_in[layer][:, 3 * ATTN_WIDTH:]], axis=1).astype(BF16),
        "w_vt": w_in[layer][:, 2 * ATTN_WIDTH:3 * ATTN_WIDTH].T.astype(BF16),
        "lq1": vec(lambda_q1[layer]), "lk1": vec(lambda_k1[layer]),
        "lq2": vec(lambda_q2[layer]), "lk2": vec(lambda_k2[layer]),
        "g_subln": g_subln[layer].reshape(-1, 1).astype(F32), "d": vec(ssm_d[layer]),
        "w_glu": w_glu[layer].astype(BF16), "b_glu": vec(b_glu[layer]),
        "g_ssm": vec(g_ssm_out[layer]), "w_out": w_out[layer].astype(BF16),
        "g_ffn": vec(g_ffn_norm[layer]), "w_up": w_up[layer].astype(BF16),
        "conv_w": conv_w[layer].astype(F32), "conv_b": vec(conv_b[layer]),
        "w_down": w_down[layer].astype(BF16), "g_final": vec(g_final),
    }
    f32 = lambda t: t[layer].astype(F32)
    ssm_w = _ssm_weights(f32(ssm_a_re), f32(ssm_a_im), f32(ssm_log_step), f32(ssm_b_re),
                         f32(ssm_b_im), f32(ssm_c_re), f32(ssm_c_im))
    perms = _relayout_constants()
    return _trunk(x_prompt, p, ssm_w, perms), _trunk(x_sample, p, ssm_w, perms)
```

```python
import functools
import math

import jax
import jax.numpy as jnp
import numpy as np
from jax import lax
from jax.experimental import pallas as pl
from jax.experimental.pallas import tpu as pltpu

F32 = jnp.float32
BF16 = jnp.bfloat16

D_MODEL = 1024
ATTN_WIDTH = 512
SSM_WIDTH = 512
N_HEADS = 4
HEAD_DIM = 64
HEAD_WIDTH = 2 * HEAD_DIM
SSM_GROUP = 16
N_SSM_GROUPS = 32
N_GROUP_PAIRS = N_SSM_GROUPS // 2
STATE = 64
PAIR_STATE = 2 * STATE
D_FF = 2816
NORM_EPS = 1e-6
SUBLN_EPS = 1e-5
LAM_INIT = 0.8 - 0.6 * math.exp(-0.3 * 0)
LOG2E = math.log2(math.e)
N_BIAS_LANES = 3
ATTN_UNDERFLOW_LOG2 = 152.0
ATTN_NORM_SLACK = 1.001
ATTN_WINDOW_MIN_BLOCKS = 8

SSM_CHUNK = 32
CHUNK_W = SSM_CHUNK * SSM_GROUP
LAG_W = 2 * CHUNK_W
POW_BITS = SSM_CHUNK.bit_length()
N_SLABS = 4

V7X_MXU_WIDTH = 256
V7X_SUBLANES = 8
V7X_BF16_ROWS = 16
V7X_VMEM_LIMIT = 56 * 1024 * 1024

HI = lax.Precision.HIGHEST


def _params(sem, vmem=V7X_VMEM_LIMIT, flags=None):
    return pltpu.CompilerParams(dimension_semantics=sem, vmem_limit_bytes=vmem, flags=flags)


def _rms(x, g, eps):
    return x * lax.rsqrt(jnp.mean(x * x, axis=-1, keepdims=True) + eps) * g


RELAYOUT_ROWS = 512
CHUNKS_PER_TILE = RELAYOUT_ROWS // SSM_CHUNK
LANE = 128
STEP_LO = LANE // SSM_GROUP
STEP_HI = SSM_CHUNK // STEP_LO
LANE_TILES = SSM_WIDTH // LANE
PERM_W = STEP_LO * LANE


def _relayout_constants():
    rows = np.zeros((RELAYOUT_ROWS, RELAYOUT_ROWS), np.float32)
    for c in range(CHUNKS_PER_TILE):
        for s in range(SSM_CHUNK):
            rows[s * CHUNKS_PER_TILE + c, c * SSM_CHUNK + s] = 1.0
    lanes = np.zeros((PERM_W, PERM_W), np.float32)
    for s_lo in range(STEP_LO):
        for g8 in range(STEP_LO):
            for i in range(SSM_GROUP):
                lanes[s_lo * LANE + g8 * SSM_GROUP + i, g8 * LANE + s_lo * SSM_GROUP + i] = 1.0
    as_bf = lambda m: jnp.asarray(m, BF16)
    return as_bf(rows), as_bf(lanes), as_bf(lanes.T), as_bf(rows.T)


def _pair_lanes(lane_tile, g8, step_hi):
    group = lane_tile * STEP_LO + g8
    return group // 2, (group % 2) * CHUNK_W + step_hi * LANE


def _to_chunk_layout(u, rowperm_ref, laneperm_ref, x_ref):
    nct = CHUNKS_PER_TILE
    up = jnp.dot(rowperm_ref[...], u.astype(BF16), preferred_element_type=F32).astype(BF16)
    stacked = jnp.concatenate(
        [jnp.concatenate([up[(hi * STEP_LO + lo) * nct:(hi * STEP_LO + lo + 1) * nct,
                             lt * LANE:(lt + 1) * LANE] for lo in range(STEP_LO)], axis=1)
         for lt in range(LANE_TILES) for hi in range(STEP_HI)], axis=0)
    out = jnp.dot(stacked, laneperm_ref[...], preferred_element_type=F32).astype(BF16)
    for lt in range(LANE_TILES):
        for hi in range(STEP_HI):
            r0 = (lt * STEP_HI + hi) * nct
            for g8 in range(STEP_LO):
                pair, lane0 = _pair_lanes(lt, g8, hi)
                x_ref[pair, :, pl.ds(lane0, LANE)] = out[r0:r0 + nct, g8 * LANE:(g8 + 1) * LANE]


def _from_chunk_layout(y_ref, laneperm_t_ref, rowperm_t_ref):
    nct = CHUNKS_PER_TILE
    pieces = []
    for lt in range(LANE_TILES):
        for hi in range(STEP_HI):
            row = []
            for g8 in range(STEP_LO):
                pair, lane0 = _pair_lanes(lt, g8, hi)
                row.append(y_ref[pair, :, pl.ds(lane0, LANE)])
            pieces.append(jnp.concatenate(row, axis=1))
    stacked = jnp.concatenate(pieces, axis=0)
    high = stacked.astype(BF16)
    low = (stacked - high.astype(F32)).astype(BF16)
    z2 = jnp.dot(jnp.concatenate([high, low], axis=0), laneperm_t_ref[...],
                 preferred_element_type=F32)
    z = z2[:LANE_TILES * STEP_HI * nct] + z2[LANE_TILES * STEP_HI * nct:]
    perm = jnp.concatenate(
        [jnp.concatenate([z[(lt * STEP_HI + hi) * nct:(lt * STEP_HI + hi + 1) * nct,
                            lo * LANE:(lo + 1) * LANE] for lt in range(LANE_TILES)], axis=1)
         for hi in range(STEP_HI) for lo in range(STEP_LO)], axis=0)
    high = perm.astype(BF16)
    low = (perm - high.astype(F32)).astype(BF16)
    y2 = jnp.dot(rowperm_t_ref[...], jnp.concatenate([high, low], axis=1),
                 preferred_element_type=F32)
    return y2[:, :SSM_WIDTH] + y2[:, SSM_WIDTH:]


def _inproj_body(x_ref, g_ref, wqku_ref, wvt_ref, rowperm_ref, laneperm_ref,
                 q_ref, k_ref, vt_ref, u_ref, xc_ref):
    n = _rms(x_ref[...], g_ref[...], NORM_EPS).astype(BF16)
    proj = jnp.dot(n, wqku_ref[...], preferred_element_type=F32)
    a = ATTN_WIDTH
    q_ref[...] = (proj[:, :a] * (LOG2E * HEAD_DIM ** -0.5)).astype(BF16)
    k_ref[...] = proj[:, a:2 * a].astype(BF16)
    u = proj[:, 2 * a:]
    u_ref[...] = u
    _to_chunk_layout(u, rowperm_ref, laneperm_ref, xc_ref)
    vt_ref[...] = lax.dot_general(wvt_ref[...], n, (((1,), (1,)), ((), ())),
                                  preferred_element_type=F32).astype(BF16)


def _inproj(x2, g, w_qku, w_vt, rowperm, laneperm):
    n_tok = x2.shape[0]
    tm = RELAYOUT_ROWS
    row = lambda w: pl.BlockSpec((tm, w), lambda i: (i, 0))
    full = lambda x: pl.BlockSpec(x.shape, lambda i: (0, 0))
    chunk_block = pl.BlockSpec((N_GROUP_PAIRS, CHUNKS_PER_TILE, 2 * CHUNK_W), lambda i: (0, i, 0))
    return pl.pallas_call(
        _inproj_body,
        grid=(n_tok // tm,),
        in_specs=[row(D_MODEL), full(g), full(w_qku), full(w_vt), full(rowperm), full(laneperm)],
        out_specs=[row(ATTN_WIDTH), row(ATTN_WIDTH),
                   pl.BlockSpec((ATTN_WIDTH, tm), lambda i: (0, i)), row(SSM_WIDTH), chunk_block],
        out_shape=[jax.ShapeDtypeStruct((n_tok, ATTN_WIDTH), BF16),
                   jax.ShapeDtypeStruct((n_tok, ATTN_WIDTH), BF16),
                   jax.ShapeDtypeStruct((ATTN_WIDTH, n_tok), BF16),
                   jax.ShapeDtypeStruct((n_tok, SSM_WIDTH), F32),
                   jax.ShapeDtypeStruct((N_GROUP_PAIRS, n_tok // SSM_CHUNK, 2 * CHUNK_W), BF16)],
        compiler_params=_params(("parallel",)),
        name="inproj",
    )(x2, g, w_qku, w_vt, rowperm, laneperm)


def _max_half_norm_sq(x):
    sq = x.astype(F32)
    sq = sq * sq
    lane = lax.broadcasted_iota(jnp.int32, sq.shape, 1)
    first = jnp.sum(jnp.where(lane < HEAD_DIM, sq, 0.0), axis=1, keepdims=True)
    second = jnp.sum(jnp.where(lane < HEAD_DIM, 0.0, sq), axis=1, keepdims=True)
    return jnp.max(jnp.maximum(first, second), axis=0, keepdims=True)


def _attn_body(q_ref, k_ref, vt_ref, lq1_ref, lk1_ref, lq2_ref, lk2_ref, g_ref, o_ref,
               qa_ref, kaug_ref, s_ref, p_ref, mx_ref, beta_ref, alpha_ref, m_ref, acc_ref,
               knorm_ref, *, tq, tk, nk, windowed):
    h = pl.program_id(1)
    i = pl.program_id(2)
    tq2 = 2 * tq
    sigma = jnp.float32(0.0)
    for head in range(N_HEADS):
        sigma = jnp.where(h == head,
                          jnp.float32(LOG2E * 2.0 ** (-8.0 * (head + 1) / N_HEADS)), sigma)
    q0 = i * tq
    jd = q0 // tk

    q = q_ref[0]
    lane = lax.broadcasted_iota(jnp.int32, q.shape, 1)
    zero = jnp.zeros_like(q)
    ones3 = jnp.where(lane < N_BIAS_LANES, 1.0, 0.0).astype(BF16)
    qa_ref[:tq, :HEAD_WIDTH] = jnp.where(lane < HEAD_DIM, q, zero)
    qa_ref[tq:, :HEAD_WIDTH] = jnp.where(lane < HEAD_DIM, zero, q)
    qa_ref[:tq, HEAD_WIDTH:] = ones3
    qa_ref[tq:, HEAD_WIDTH:] = ones3

    @pl.when(i == 0)
    def _():
        koff = lax.broadcasted_iota(jnp.int32, (tk, HEAD_WIDTH), 0).astype(F32) * sigma
        klane = lax.broadcasted_iota(jnp.int32, (tk, HEAD_WIDTH), 1)
        hi = koff.astype(BF16).astype(F32)
        mid = (koff - hi).astype(BF16).astype(F32)
        lo = koff - hi - mid
        ktile = jnp.where(klane == 0, hi,
                          jnp.where(klane == 1, mid, jnp.where(klane == 2, lo, 0.0)))
        kaug_ref[0] = ktile.astype(BF16)
        kaug_ref[1] = (-ktile).astype(BF16)

        if windowed:
            def knorm(c, best):
                rows = k_ref[0, pl.ds(pl.multiple_of(c * tk, tk), tk), :]
                return jnp.maximum(best, _max_half_norm_sq(rows))

            knorm_ref[...] = lax.fori_loop(0, nk, knorm, jnp.zeros((1, 1), F32))

    if windowed:
        qk = jnp.sqrt(_max_half_norm_sq(q) * knorm_ref[...]) * ATTN_NORM_SLACK
        reach = (2.0 * qk + ATTN_UNDERFLOW_LOG2) / sigma
        radius = jnp.maximum(jnp.floor((reach - 1.0) / tk) + 1.0, 0.0)
        radius = jnp.minimum(radius, float(nk)).astype(jnp.int32)[0, 0]
        jlo = jnp.maximum(jd - radius, 0)
        jhi = jnp.minimum(jd + radius, nk - 1)
        odd = lax.rem(jhi - jlo + 1, 2) == 1
        grow_hi = jnp.logical_and(odd, jhi < nk - 1)
        grow_lo = jnp.logical_and(odd, jhi >= nk - 1)
        jhi = jnp.where(grow_hi, jhi + 1, jhi)
        jlo = jnp.where(grow_lo, jlo - 1, jlo)
        nb = jhi - jlo + 1
    else:
        jlo, nb = 0, nk

    ql = lax.broadcasted_iota(jnp.int32, (1, tq2), 1)
    qpos = (q0 + jnp.where(ql < tq, ql, ql - tq)).astype(F32)
    ones_rows = jnp.where(
        lax.broadcasted_iota(jnp.int32, (V7X_BF16_ROWS, tk), 0) == 0, 1.0, 0.0).astype(BF16)

    def block_of(n):
        j = jlo + n - 1
        return jnp.where(n == 0, jd, jnp.where(j < jd, j, j + 1))

    lane_chunks = [pl.ds(c * V7X_MXU_WIDTH, V7X_MXU_WIDTH) for c in range(tq2 // V7X_MXU_WIDTH)]
    kk = lax.broadcasted_iota(jnp.int32, (tk, V7X_MXU_WIDTH), 0)
    qq = lax.broadcasted_iota(jnp.int32, (tk, V7X_MXU_WIDTH), 1)

    def keys_of(n):
        j = block_of(n)
        right = (j > jd).astype(jnp.int32)
        start = pl.multiple_of(j * tk, tk)
        ka = jnp.concatenate([k_ref[0, pl.ds(start, tk), :], kaug_ref[right]], axis=1)
        sgn = jnp.where(j > jd, jnp.float32(-1.0), jnp.float32(1.0))
        beta = (((j * tk).astype(F32) - qpos) * sigma) * sgn
        return ka, beta

    def scores_chunk(ka, beta, slot, c, diagonal=False):
        cs = lane_chunks[c]
        st = lax.dot_general(ka, qa_ref[cs, :], (((1,), (1,)), ((), ())),
                             preferred_element_type=F32)
        if diagonal:
            qoff = (c * V7X_MXU_WIDTH) % tq
            st = st + (jnp.maximum(kk - qq + (jd * tk - q0 - qoff), 0).astype(F32)
                       * sigma) * -2.0
        s_ref[slot, :, cs] = st
        mx_ref[slot, :, cs] = jnp.max(st, axis=0, keepdims=True) + beta[:, c * V7X_MXU_WIDTH:
                                                                      (c + 1) * V7X_MXU_WIDTH]

    def softmax_chunk(slot, c):
        cs = lane_chunks[c]
        m_prev = m_ref[:, cs]
        m_new = jnp.maximum(m_prev, mx_ref[slot, :, cs])
        alpha_ref[slot, :, cs] = jnp.exp2(m_prev - m_new)
        p_ref[slot, :, cs] = jnp.exp2(s_ref[slot, :, cs]
                                      - (m_new - beta_ref[slot, :, cs])).astype(BF16)
        m_ref[:, cs] = m_new

    def pv_chunk(vt, slot, c):
        cs = lane_chunks[c]
        acc_ref[:, cs] = alpha_ref[slot, :, cs] * acc_ref[:, cs] + jnp.dot(
            vt, p_ref[slot, :, cs], preferred_element_type=F32)

    def values_of(n):
        start = pl.multiple_of(block_of(n) * tk, tk)
        return jnp.concatenate([vt_ref[:, pl.ds(start, tk)], ones_rows], axis=0)

    def pipeline_step(n_scores, n_softmax, n_pv, diagonal=False):
        if n_scores is not None:
            ka, beta = keys_of(n_scores[0])
            beta_ref[n_scores[1]] = beta
        if n_pv is not None:
            vt = values_of(n_pv[0])
        for c in range(len(lane_chunks)):
            if n_softmax is not None:
                softmax_chunk(n_softmax, c)
            if n_scores is not None:
                scores_chunk(ka, beta, n_scores[1], c, diagonal)
            if n_pv is not None:
                pv_chunk(vt, n_pv[1], c)

    m_ref[...] = jnp.full(m_ref.shape, -jnp.inf, F32)
    acc_ref[...] = jnp.zeros(acc_ref.shape, F32)
    pipeline_step((0, 0), None, None, diagonal=True)
    pipeline_step((1, 1), 0, None)

    def pair(t, carry):
        n = 2 * t
        pipeline_step((n + 2, 0), 1, (n, 0))
        pipeline_step((n + 3, 1), 0, (n + 1, 1))
        return carry

    if windowed:
        lax.fori_loop(0, lax.div(nb, 2) - 1, pair, 0)
    else:
        n_pairs = nk // 2 - 1
        lax.fori_loop(0, n_pairs // 2, lambda t, c: pair(2 * t + 1, pair(2 * t, c)), 0)
        if n_pairs % 2:
            pair(n_pairs - 1, 0)
    pipeline_step(None, 1, (nb - 2, 0))
    pipeline_step(None, None, (nb - 1, 1))

    lam = (jnp.exp(jnp.sum(lq1_ref[...] * lk1_ref[...], axis=-1, keepdims=True))
           - jnp.exp(jnp.sum(lq2_ref[...] * lk2_ref[...], axis=-1, keepdims=True))
           + LAM_INIT)
    acc = acc_ref[...]
    o = acc[:HEAD_WIDTH, :] * (1.0 / acc[HEAD_WIDTH:HEAD_WIDTH + 1, :])
    at = o[:, :tq] - lam * o[:, tq:]
    ms = jnp.mean(at * at, axis=0, keepdims=True)
    at = at * lax.rsqrt(ms + SUBLN_EPS) * g_ref[...] * (1.0 - LAM_INIT)
    o_ref[0] = at.T.astype(o_ref.dtype)


def _attention(q, k, vt, lq1, lk1, lq2, lk2, g_subln_col, tq, tk):
    b, s, _ = q.shape
    nk = s // tk
    assert tk % tq == 0 and nk % 2 == 0 and nk >= 2
    vec = lambda n: pl.BlockSpec((1, n), lambda bi, hi, qi: (0, 0))
    return pl.pallas_call(
        functools.partial(_attn_body, tq=tq, tk=tk, nk=nk,
                          windowed=nk > ATTN_WINDOW_MIN_BLOCKS),
        grid=(b, N_HEADS, s // tq),
        in_specs=[pl.BlockSpec((1, tq, HEAD_WIDTH), lambda bi, hi, qi: (bi, qi, hi)),
                  pl.BlockSpec((1, s, HEAD_WIDTH), lambda bi, hi, qi: (bi, 0, hi)),
                  pl.BlockSpec((HEAD_WIDTH, s), lambda bi, hi, qi: (hi, bi)),
                  vec(HEAD_DIM), vec(HEAD_DIM), vec(HEAD_DIM), vec(HEAD_DIM),
                  pl.BlockSpec((HEAD_WIDTH, 1), lambda bi, hi, qi: (0, 0))],
        out_specs=pl.BlockSpec((1, tq, HEAD_WIDTH), lambda bi, hi, qi: (bi, qi, hi)),
        out_shape=jax.ShapeDtypeStruct((b, s, ATTN_WIDTH), BF16),
        scratch_shapes=[pltpu.VMEM((2 * tq, 2 * HEAD_WIDTH), BF16),
                        pltpu.VMEM((2, tk, HEAD_WIDTH), BF16),
                        pltpu.VMEM((2, tk, 2 * tq), F32),
                        pltpu.VMEM((2, tk, 2 * tq), BF16),
                        pltpu.VMEM((2, 1, 2 * tq), F32),
                        pltpu.VMEM((2, 1, 2 * tq), F32),
                        pltpu.VMEM((2, 1, 2 * tq), F32),
                        pltpu.VMEM((1, 2 * tq), F32),
                        pltpu.VMEM((HEAD_WIDTH + V7X_BF16_ROWS, 2 * tq), F32),
                        pltpu.VMEM((1, 1), F32)],
        compiler_params=_params(("parallel", "parallel", "arbitrary")),
        name="diff_attention",
    )(q, k, vt, lq1, lk1, lq2, lk2, g_subln_col)


def _cpow(lre, lim, n):
    shape = jnp.broadcast_shapes(lre.shape, n.shape)
    pre = jnp.ones(shape, F32)
    pim = jnp.zeros(shape, F32)
    bre, bim = lre, lim
    for bit in range(POW_BITS):
        on = ((n >> bit) & 1) == 1
        mre = jnp.where(on, bre, 1.0)
        mim = jnp.where(on, bim, 0.0)
        pre, pim = pre * mre - pim * mim, pre * mim + pim * mre
        bre, bim = bre * bre - bim * bim, 2.0 * bre * bim
    return pre, pim


def _discretise(a_re, a_im, log_step):
    step = jnp.exp(log_step)
    mag = jnp.exp(a_re * step)
    lre = mag * jnp.cos(a_im * step)
    lim = mag * jnp.sin(a_im * step)
    den = a_re * a_re + a_im * a_im
    nr = lre - 1.0
    fre = (nr * a_re + lim * a_im) / den
    fim = (lim * a_re - nr * a_im) / den
    return lre, lim, fre, fim


def _ssm_weights_body(arow_re, arow_im, acol_re, acol_im, lsrow, lscol, bt_re, bt_im,
                      ct_re, ct_im, wt_ref, ws_ref, wo_ref, at_ref):
    t, cw = SSM_CHUNK, CHUNK_W
    lane_pair = lax.broadcasted_iota(jnp.int32, (1, PAIR_STATE), 1) // STATE
    row_pair = lax.broadcasted_iota(jnp.int32, (PAIR_STATE, 1), 0) // STATE

    ch = lax.broadcasted_iota(jnp.int32, (SSM_GROUP, LAG_W), 0)
    ln = lax.broadcasted_iota(jnp.int32, (SSM_GROUP, LAG_W), 1)
    tile_lag = jnp.where((ln % SSM_GROUP) == ch, 1.0, 0.0).astype(F32)

    step_rows = lax.broadcasted_iota(jnp.int32, (cw, 1), 0) // SSM_GROUP
    slot = lax.broadcasted_iota(jnp.int32, (1, LAG_W), 1) // SSM_GROUP
    step_lanes = lax.broadcasted_iota(jnp.int32, (1, cw), 1) // SSM_GROUP

    zt = [jnp.zeros((SSM_GROUP, LAG_W), F32) for _ in range(2)]
    for d in range(2):
        lre, lim, fre, fim = _discretise(arow_re[d, 0], arow_im[d, 0], lsrow[d, 0])
        bre, bim = bt_re[d, 0], bt_im[d, 0]
        bbre = fre * bre - fim * bim
        bbim = fre * bim + fim * bre
        expo = (t - 1 - step_rows) if d == 0 else step_rows
        pre, pim = _cpow(lre, lim, expo)
        tbre = jnp.tile(bbre, (t, 1))
        tbim = jnp.tile(bbim, (t, 1))
        inj_re = pre * tbre - pim * tbim
        inj_im = pre * tbim + pim * tbre
        for g2 in range(2):
            keep = lane_pair == g2
            rows = pl.ds(g2 * cw, cw)
            ws_ref[0, rows, pl.ds((2 * d) * PAIR_STATE, PAIR_STATE)] = (
                jnp.where(keep, inj_re, 0.0).astype(ws_ref.dtype))
            ws_ref[0, rows, pl.ds((2 * d + 1) * PAIR_STATE, PAIR_STATE)] = (
                jnp.where(keep, inj_im, 0.0).astype(ws_ref.dtype))
        dre, dim_ = _cpow(lre, lim, jnp.full((1, 1), t, jnp.int32))
        at_ref[0, pl.ds(2 * d, 1), :] = dre
        at_ref[0, pl.ds(2 * d + 1, 1), :] = dim_

        cre, cim, _, _ = _discretise(acol_re[d, 0], acol_im[d, 0], lscol[d, 0])
        ctl_re = jnp.dot(ct_re[d, 0], tile_lag, precision=HI, preferred_element_type=F32)
        ctl_im = jnp.dot(ct_im[d, 0], tile_lag, precision=HI, preferred_element_type=F32)
        lag = (slot - (t - 1)) if d == 0 else ((t - 1) - slot)
        valid = (lag >= 0) & (slot < 2 * t - 1)
        qre, qim = _cpow(cre, cim, jnp.maximum(lag, 0))
        r_re = jnp.where(valid, qre * ctl_re - qim * ctl_im, 0.0)
        r_im = jnp.where(valid, qre * ctl_im + qim * ctl_re, 0.0)
        for g2 in range(2):
            keep = lane_pair == g2
            zt[g2] = (zt[g2]
                      + jnp.dot(jnp.where(keep, bbre, 0.0), r_re, precision=HI,
                                preferred_element_type=F32)
                      - jnp.dot(jnp.where(keep, bbim, 0.0), r_im, precision=HI,
                                preferred_element_type=F32))

        out_pow = (step_lanes + 1) if d == 0 else (t - step_lanes)
        ore, oim = _cpow(cre, cim, out_pow)
        c_re = ctl_re[:, :cw]
        c_im = ctl_im[:, :cw]
        rd_re = ore * c_re - oim * c_im
        rd_im = -(ore * c_im + oim * c_re)
        for g2 in range(2):
            keep = row_pair == g2
            cols = pl.ds(g2 * cw, cw)
            wo_ref[0, pl.ds((2 * d) * PAIR_STATE, PAIR_STATE), cols] = (
                jnp.where(keep, rd_re, 0.0).astype(wo_ref.dtype))
            wo_ref[0, pl.ds((2 * d + 1) * PAIR_STATE, PAIR_STATE), cols] = (
                jnp.where(keep, rd_im, 0.0).astype(wo_ref.dtype))

    for g2 in range(2):
        for s in range(t):
            off = (t - 1 - s) * SSM_GROUP
            shifted = zt[g2] if off == 0 else pltpu.roll(zt[g2], LAG_W - off, 1)
            wt_ref[0, g2, pl.ds(s * SSM_GROUP, SSM_GROUP), :] = (
                shifted[:, :cw].astype(wt_ref.dtype))


def _ssm_weights(a_re, a_im, log_step, b_re, b_im, c_re, c_im):
    np_, g2p = N_GROUP_PAIRS, PAIR_STATE
    arow = lambda a: a.reshape(2, np_, 1, g2p)
    acol = lambda a: a.reshape(2, np_, g2p, 1)
    ls = jnp.repeat(log_step, STATE, axis=-1)
    bt = lambda w: (w.reshape(2, np_, 2, STATE, SSM_GROUP)
                    .transpose(0, 1, 4, 2, 3).reshape(2, np_, SSM_GROUP, g2p))
    ct = lambda w: (w.reshape(2, np_, 2, SSM_GROUP, STATE)
                    .transpose(0, 1, 2, 4, 3).reshape(2, np_, g2p, SSM_GROUP))
    ins = [arow(a_re), arow(a_im), acol(a_re), acol(a_im), arow(ls), acol(ls),
           bt(b_re), bt(b_im), ct(c_re), ct(c_im)]
    spec = lambda x: pl.BlockSpec((2, 1) + x.shape[2:], lambda k: (0, k, 0, 0))
    return pl.pallas_call(
        _ssm_weights_body,
        grid=(np_,),
        in_specs=[spec(x) for x in ins],
        out_specs=[pl.BlockSpec((1, 2, CHUNK_W, CHUNK_W), lambda k: (k, 0, 0, 0)),
                   pl.BlockSpec((1, 2 * CHUNK_W, N_SLABS * g2p), lambda k: (k, 0, 0)),
                   pl.BlockSpec((1, N_SLABS * g2p, 2 * CHUNK_W), lambda k: (k, 0, 0)),
                   pl.BlockSpec((1, N_SLABS, g2p), lambda k: (k, 0, 0))],
        out_shape=[jax.ShapeDtypeStruct((np_, 2, CHUNK_W, CHUNK_W), BF16),
                   jax.ShapeDtypeStruct((np_, 2 * CHUNK_W, N_SLABS * g2p), BF16),
                   jax.ShapeDtypeStruct((np_, N_SLABS * g2p, 2 * CHUNK_W), BF16),
                   jax.ShapeDtypeStruct((np_, N_SLABS, g2p), F32)],
        compiler_params=_params(("parallel",)),
        name="ssm_weights",
    )(*ins)


def _ssm_inject_body(x_ref, ws_ref, s_ref):
    s = jnp.dot(x_ref[0], ws_ref[0], preferred_element_type=F32)
    for slab in range(N_SLABS):
        s_ref[slab] = s[:, slab * PAIR_STATE:(slab + 1) * PAIR_STATE]


def _ssm_inject(x, ws):
    np_, r, _ = x.shape
    return pl.pallas_call(
        _ssm_inject_body,
        grid=(np_,),
        in_specs=[pl.BlockSpec((1, r, 2 * CHUNK_W), lambda k: (k, 0, 0)),
                  pl.BlockSpec((1,) + ws.shape[1:], lambda k: (k, 0, 0))],
        out_specs=pl.BlockSpec((N_SLABS, r, PAIR_STATE), lambda k: (0, 0, k)),
        out_shape=jax.ShapeDtypeStruct((N_SLABS, r, np_ * PAIR_STATE), F32),
        compiler_params=_params(("parallel",)),
        name="ssm_inject",
    )(x, ws)


def _ssm_scan_body(s_ref, at_ref, hin_ref, *, tiles_per_seq, n_tiles):
    sub = V7X_SUBLANES
    lanes = s_ref.shape[-1]
    a = [at_ref[pl.ds(slab, 1), :] for slab in range(N_SLABS)]

    def sweep(d, tile, carry):
        edge = 0 if d == 0 else tiles_per_seq - 1
        keep = jnp.where(lax.rem(tile, tiles_per_seq) == edge,
                         jnp.float32(0.0), jnp.float32(1.0))
        hre, him = carry[0] * keep, carry[1] * keep
        are, aim = a[2 * d], a[2 * d + 1]
        row0 = pl.multiple_of(tile * V7X_SUBLANES, V7X_SUBLANES)
        sre = s_ref[2 * d, pl.ds(row0, V7X_SUBLANES), :]
        sim = s_ref[2 * d + 1, pl.ds(row0, V7X_SUBLANES), :]
        out_re = [None] * sub
        out_im = [None] * sub
        order = range(sub) if d == 0 else range(sub - 1, -1, -1)
        for q in order:
            out_re[q], out_im[q] = hre, him
            xr = sre[q:q + 1]
            xi = sim[q:q + 1]
            hre, him = are * hre - aim * him + xr, are * him + aim * hre + xi
        hin_ref[2 * d, pl.ds(row0, V7X_SUBLANES), :] = jnp.concatenate(out_re, axis=0)
        hin_ref[2 * d + 1, pl.ds(row0, V7X_SUBLANES), :] = jnp.concatenate(out_im, axis=0)
        return hre, him

    zero = (jnp.zeros((1, lanes), F32), jnp.zeros((1, lanes), F32))
    lax.fori_loop(0, n_tiles, lambda i, c: sweep(0, i, c), zero)
    lax.fori_loop(0, n_tiles, lambda i, c: sweep(1, n_tiles - 1 - i, c), zero)


def _ssm_scan(s, at, chunks_per_seq, lane_block):
    _, r, lanes = s.shape
    assert chunks_per_seq % V7X_SUBLANES == 0 and r % chunks_per_seq == 0
    return pl.pallas_call(
        functools.partial(_ssm_scan_body, tiles_per_seq=chunks_per_seq // V7X_SUBLANES,
                          n_tiles=r // V7X_SUBLANES),
        grid=(lanes // lane_block,),
        in_specs=[pl.BlockSpec((N_SLABS, r, lane_block), lambda k: (0, 0, k)),
                  pl.BlockSpec((N_SLABS, lane_block), lambda k: (0, k))],
        out_specs=pl.BlockSpec((N_SLABS, r, lane_block), lambda k: (0, 0, k)),
        out_shape=jax.ShapeDtypeStruct(s.shape, F32),
        compiler_params=_params(("parallel",)),
        name="ssm_scan",
    )(s, at)


def _ssm_mix_body(x_ref, wt_ref, hin_ref, wo_ref, y_ref):
    x = x_ref[0]
    hin = jnp.concatenate([hin_ref[slab] for slab in range(N_SLABS)], axis=1).astype(BF16)
    carried = jnp.dot(hin, wo_ref[0], preferred_element_type=F32)
    for g2 in range(2):
        cols = slice(g2 * CHUNK_W, (g2 + 1) * CHUNK_W)
        y_ref[0, :, cols] = carried[:, cols] + jnp.dot(x[:, cols], wt_ref[0, g2],
                                                       preferred_element_type=F32)


def _ssm_mix(x, wt, hin, wo):
    np_, r, _ = x.shape
    return pl.pallas_call(
        _ssm_mix_body,
        grid=(np_,),
        in_specs=[pl.BlockSpec((1, r, 2 * CHUNK_W), lambda k: (k, 0, 0)),
                  pl.BlockSpec((1,) + wt.shape[1:], lambda k: (k, 0, 0, 0)),
                  pl.BlockSpec((N_SLABS, r, PAIR_STATE), lambda k: (0, 0, k)),
                  pl.BlockSpec((1,) + wo.shape[1:], lambda k: (k, 0, 0))],
        out_specs=pl.BlockSpec((1, r, 2 * CHUNK_W), lambda k: (k, 0, 0)),
        out_shape=jax.ShapeDtypeStruct((np_, r, 2 * CHUNK_W), F32),
        compiler_params=_params(("parallel",)),
        name="ssm_mix",
    )(x, wt, hin, wo)


def _outproj_body(x_ref, a_ref, yc_ref, u_ref, d_ref, wg_ref, bg_ref, gs_ref, wo_ref, gf_ref,
                  laneperm_t_ref, rowperm_t_ref, x1_ref, h_ref):
    y_ssm = _from_chunk_layout(yc_ref, laneperm_t_ref, rowperm_t_ref)
    y = jax.nn.gelu(y_ssm + d_ref[...] * u_ref[...])
    gate = jnp.dot(y.astype(BF16), wg_ref[...], preferred_element_type=F32) + bg_ref[...]
    s = _rms(y * jax.nn.sigmoid(gate), gs_ref[...], NORM_EPS)
    mixed = (jnp.dot(a_ref[...], wo_ref[:ATTN_WIDTH, :], preferred_element_type=F32)
             + jnp.dot(s.astype(BF16), wo_ref[ATTN_WIDTH:, :], preferred_element_type=F32))
    x1 = x_ref[...] + mixed
    x1_ref[...] = x1
    h_ref[...] = _rms(x1, gf_ref[...], NORM_EPS).astype(h_ref.dtype)


def _outproj(x2, a2, yc, u2, d, w_glu, b_glu, g_ssm, w_out, g_ffn, laneperm_t, rowperm_t):
    n_tok = x2.shape[0]
    tm = RELAYOUT_ROWS
    row = lambda w: pl.BlockSpec((tm, w), lambda i: (i, 0))
    full = lambda x: pl.BlockSpec(x.shape, lambda i: (0, 0))
    chunk_block = pl.BlockSpec((N_GROUP_PAIRS, CHUNKS_PER_TILE, 2 * CHUNK_W), lambda i: (0, i, 0))
    return pl.pallas_call(
        _outproj_body,
        grid=(n_tok // tm,),
        in_specs=[row(D_MODEL), row(ATTN_WIDTH), chunk_block, row(SSM_WIDTH),
                  full(d), full(w_glu), full(b_glu), full(g_ssm), full(w_out), full(g_ffn),
                  full(laneperm_t), full(rowperm_t)],
        out_specs=[row(D_MODEL), row(D_MODEL)],
        out_shape=[jax.ShapeDtypeStruct((n_tok, D_MODEL), F32),
                   jax.ShapeDtypeStruct((n_tok, D_MODEL), BF16)],
        compiler_params=_params(("parallel",)),
        name="outproj",
    )(x2, a2, yc, u2, d, w_glu, b_glu, g_ssm, w_out, g_ffn, laneperm_t, rowperm_t)


def _ffn_body(h_ref, hp_ref, hn_ref, x1_ref, wu_ref, cw_ref, cb_ref, wd_ref, gf_ref, o_ref,
              hcat_ref, *, tm, tiles_per_seq, ff_chunk):
    i = pl.program_id(0)
    halo = V7X_BF16_ROWS
    first = (i % tiles_per_seq) == 0
    last = (i % tiles_per_seq) == tiles_per_seq - 1
    zeros = jnp.zeros((halo, D_MODEL), hcat_ref.dtype)
    hcat_ref[:halo, :] = jnp.where(first, zeros, hp_ref[...])
    hcat_ref[halo:halo + tm, :] = h_ref[...]
    hcat_ref[halo + tm:, :] = jnp.where(last, zeros, hn_ref[...])
    hcat = hcat_ref[...]
    rows = tm + 2 * halo

    def conv(z, col0):
        cols = pl.ds(col0, ff_chunk)
        prev = pltpu.roll(z, 1, 0)[halo:halo + tm]
        nxt = pltpu.roll(z, rows - 1, 0)[halo:halo + tm]
        return (prev * cw_ref[pl.ds(0, 1), cols] + z[halo:halo + tm] * cw_ref[pl.ds(1, 1), cols]
                + nxt * cw_ref[pl.ds(2, 1), cols] + cb_ref[:, cols])

    def up(c):
        g0 = c * ff_chunk
        return (jnp.dot(hcat, wu_ref[:, pl.ds(g0, ff_chunk)], preferred_element_type=F32),
                jnp.dot(hcat, wu_ref[:, pl.ds(D_FF + g0, ff_chunk)],
                        preferred_element_type=F32))

    n_chunks = D_FF // ff_chunk
    acc = jnp.zeros((tm, D_MODEL), F32)
    z_next = up(0)
    for c in range(n_chunks):
        zg, zv = z_next
        if c + 1 < n_chunks:
            z_next = up(c + 1)
        g0 = c * ff_chunk
        act = (jax.nn.gelu(conv(zg, g0)) * conv(zv, D_FF + g0)).astype(BF16)
        acc = acc + jnp.dot(act, wd_ref[pl.ds(g0, ff_chunk), :], preferred_element_type=F32)
    o_ref[...] = _rms(x1_ref[...] + acc, gf_ref[...], NORM_EPS)


def _ffn(h2, x1, w_up, conv_w, conv_b, w_down, g_final, tm, seq, ff_chunk):
    n_tok = h2.shape[0]
    halo = V7X_BF16_ROWS
    per = tm // halo
    n_halo = n_tok // halo
    row = lambda w: pl.BlockSpec((tm, w), lambda i: (i, 0))
    once = lambda x: pl.BlockSpec(x.shape, lambda i: (0, 0), pipeline_mode=pl.Buffered(1))
    return pl.pallas_call(
        functools.partial(_ffn_body, tm=tm, tiles_per_seq=seq // tm, ff_chunk=ff_chunk),
        grid=(n_tok // tm,),
        in_specs=[row(D_MODEL),
                  pl.BlockSpec((halo, D_MODEL), lambda i: (jnp.maximum(i * per - 1, 0), 0)),
                  pl.BlockSpec((halo, D_MODEL),
                               lambda i: (jnp.minimum((i + 1) * per, n_halo - 1), 0)),
                  row(D_MODEL), once(w_up), once(conv_w), once(conv_b), once(w_down),
                  once(g_final)],
        out_specs=row(D_MODEL),
        out_shape=jax.ShapeDtypeStruct((n_tok, D_MODEL), F32),
        scratch_shapes=[pltpu.VMEM((tm + 2 * halo, D_MODEL), BF16)],
        compiler_params=_params(("parallel",)),
        name="ffn",
    )(h2, h2, h2, x1, w_up, conv_w, conv_b, w_down, g_final)


def _tile(pref, n):
    t = min(pref, n)
    assert n % t == 0, (pref, n)
    return t


def _trunk(x, p, ssm_w, perms):
    batch, seq, _ = x.shape
    n_tok = batch * seq
    x2 = x.reshape(n_tok, D_MODEL)
    assert seq % RELAYOUT_ROWS == 0
    rowperm, laneperm, laneperm_t, rowperm_t = perms

    q, k, vt, u, xc = _inproj(x2, p["g_mix"], p["w_qku"], p["w_vt"], rowperm, laneperm)

    shape3 = (batch, seq, ATTN_WIDTH)
    a = _attention(q.reshape(shape3), k.reshape(shape3), vt,
                   p["lq1"], p["lk1"], p["lq2"], p["lk2"], p["g_subln"],
                   _tile(512, seq), _tile(512, seq))

    wt, ws, wo, at = ssm_w
    s_in = _ssm_inject(xc, ws)
    at_flat = at.transpose(1, 0, 2).reshape(N_SLABS, N_GROUP_PAIRS * PAIR_STATE)
    hin = _ssm_scan(s_in, at_flat, seq // SSM_CHUNK, 4 * PAIR_STATE)
    yc = _ssm_mix(xc, wt, hin, wo)

    x1, h = _outproj(x2, a.reshape(n_tok, ATTN_WIDTH), yc, u, p["d"], p["w_glu"], p["b_glu"],
                     p["g_ssm"], p["w_out"], p["g_ffn"], laneperm_t, rowperm_t)
    out = _ffn(h, x1, p["w_up"], p["conv_w"], p["conv_b"], p["w_down"], p["g_final"],
               _tile(512, seq), seq, 1408)
    return out.reshape(batch, seq, D_MODEL)


def kernel(x_prompt, x_sample, g_mix_norm, w_in, lambda_q1, lambda_k1, lambda_q2, lambda_k2, g_subln, ssm_a_re, ssm_a_im, ssm_log_step, ssm_b_re, ssm_b_im, ssm_c_re, ssm_c_im, ssm_d, w_glu, b_glu, g_ssm_out, w_out, g_ffn_norm, w_up, conv_w, conv_b, w_down, g_final):
    layer = 0
    vec = lambda t: t.reshape(1, -1).astype(F32)
    p = {
        "g_mix": vec(g_mix_norm[layer]),
        "w_qku": jnp.concatenate([w_in[layer][:, :2 * ATTN_WIDTH],
                                  w_in[layer][:, 3 * ATTN_WIDTH:]], axis=1).astype(BF16),
        "w_vt": w_in[layer][:, 2 * ATTN_WIDTH:3 * ATTN_WIDTH].T.astype(BF16),
        "lq1": vec(lambda_q1[layer]), "lk1": vec(lambda_k1[layer]),
        "lq2": vec(lambda_q2[layer]), "lk2": vec(lambda_k2[layer]),
        "g_subln": g_subln[layer].reshape(-1, 1).astype(F32), "d": vec(ssm_d[layer]),
        "w_glu": w_glu[layer].astype(BF16), "b_glu": vec(b_glu[layer]),
        "g_ssm": vec(g_ssm_out[layer]), "w_out": w_out[layer].astype(BF16),
        "g_ffn": vec(g_ffn_norm[layer]), "w_up": w_up[layer].astype(BF16),
        "conv_w": conv_w[layer].astype(F32), "conv_b": vec(conv_b[layer]),
        "w_down": w_down[layer].astype(BF16), "g_final": vec(g_final),
    }
    f32 = lambda t: t[layer].astype(F32)
    ssm_w = _ssm_weights(f32(ssm_a_re), f32(ssm_a_im), f32(ssm_log_step), f32(ssm_b_re),
                         f32(ssm_b_im), f32(ssm_c_re), f32(ssm_c_im))
    perms = _relayout_constants()
    return _trunk(x_prompt, p, ssm_w, perms), _trunk(x_sample, p, ssm_w, perms)
```

```python
import functools
import math

import jax
import jax.numpy as jnp
import numpy as np
from jax import lax
from jax.experimental import pallas as pl
from jax.experimental.pallas import tpu as pltpu

F32 = jnp.float32
BF16 = jnp.bfloat16

D_MODEL = 1024
ATTN_WIDTH = 512
SSM_WIDTH = 512
N_HEADS = 4
HEAD_DIM = 64
HEAD_WIDTH = 2 * HEAD_DIM
SSM_GROUP = 16
N_SSM_GROUPS = 32
N_GROUP_PAIRS = N_SSM_GROUPS // 2
STATE = 64
PAIR_STATE = 2 * STATE
D_FF = 2816
NORM_EPS = 1e-6
SUBLN_EPS = 1e-5
LAM_INIT = 0.8 - 0.6 * math.exp(-0.3 * 0)
LOG2E = math.log2(math.e)
N_BIAS_LANES = 3
ATTN_UNDERFLOW_LOG2 = 152.0
ATTN_NORM_SLACK = 1.001
ATTN_WINDOW_MIN_BLOCKS = 8

SSM_CHUNK = 32
CHUNK_W = SSM_CHUNK * SSM_GROUP
LAG_W = 2 * CHUNK_W
POW_BITS = SSM_CHUNK.bit_length()
N_SLABS = 4

V7X_MXU_WIDTH = 256
V7X_SUBLANES = 8
V7X_BF16_ROWS = 16
V7X_VMEM_LIMIT = 56 * 1024 * 1024

HI = lax.Precision.HIGHEST


def _params(sem, vmem=V7X_VMEM_LIMIT, flags=None):
    return pltpu.CompilerParams(dimension_semantics=sem, vmem_limit_bytes=vmem, flags=flags)


def _rms(x, g, eps):
    return x * lax.rsqrt(jnp.mean(x * x, axis=-1, keepdims=True) + eps) * g


RELAYOUT_ROWS = 512
CHUNKS_PER_TILE = RELAYOUT_ROWS // SSM_CHUNK
LANE = 128
STEP_LO = LANE // SSM_GROUP
STEP_HI = SSM_CHUNK // STEP_LO
LANE_TILES = SSM_WIDTH // LANE
PERM_W = STEP_LO * LANE


def _relayout_constants():
    rows = np.zeros((RELAYOUT_ROWS, RELAYOUT_ROWS), np.float32)
    for c in range(CHUNKS_PER_TILE):
        for s in range(SSM_CHUNK):
            rows[s * CHUNKS_PER_TILE + c, c * SSM_CHUNK + s] = 1.0
    lanes = np.zeros((PERM_W, PERM_W), np.float32)
    for s_lo in range(STEP_LO):
        for g8 in range(STEP_LO):
            for i in range(SSM_GROUP):
                lanes[s_lo * LANE + g8 * SSM_GROUP + i, g8 * LANE + s_lo * SSM_GROUP + i] = 1.0
    as_bf = lambda m: jnp.asarray(m, BF16)
    return as_bf(rows), as_bf(lanes), as_bf(lanes.T), as_bf(rows.T)


def _pair_lanes(lane_tile, g8, step_hi):
    group = lane_tile * STEP_LO + g8
    return group // 2, (group % 2) * CHUNK_W + step_hi * LANE


def _to_chunk_layout(u, rowperm_ref, laneperm_ref, x_ref):
    nct = CHUNKS_PER_TILE
    up = jnp.dot(rowperm_ref[...], u.astype(BF16), preferred_element_type=F32).astype(BF16)
    stacked = jnp.concatenate(
        [jnp.concatenate([up[(hi * STEP_LO + lo) * nct:(hi * STEP_LO + lo + 1) * nct,
                             lt * LANE:(lt + 1) * LANE] for lo in range(STEP_LO)], axis=1)
         for lt in range(LANE_TILES) for hi in range(STEP_HI)], axis=0)
    out = jnp.dot(stacked, laneperm_ref[...], preferred_element_type=F32).astype(BF16)
    for lt in range(LANE_TILES):
        for hi in range(STEP_HI):
            r0 = (lt * STEP_HI + hi) * nct
            for g8 in range(STEP_LO):
                pair, lane0 = _pair_lanes(lt, g8, hi)
                x_ref[pair, :, pl.ds(lane0, LANE)] = out[r0:r0 + nct, g8 * LANE:(g8 + 1) * LANE]


def _from_chunk_layout(y_ref, laneperm_t_ref, rowperm_t_ref):
    nct = CHUNKS_PER_TILE
    pieces = []
    for lt in range(LANE_TILES):
        for hi in range(STEP_HI):
            row = []
            for g8 in range(STEP_LO):
                pair, lane0 = _pair_lanes(lt, g8, hi)
                row.append(y_ref[pair, :, pl.ds(lane0, LANE)])
            pieces.append(jnp.concatenate(row, axis=1))
    stacked = jnp.concatenate(pieces, axis=0)
    z = jnp.dot(stacked.astype(BF16), laneperm_t_ref[...],
                preferred_element_type=F32).astype(BF16)
    perm = jnp.concatenate(
        [jnp.concatenate([z[(lt * STEP_HI + hi) * nct:(lt * STEP_HI + hi + 1) * nct,
                            lo * LANE:(lo + 1) * LANE] for lt in range(LANE_TILES)], axis=1)
         for hi in range(STEP_HI) for lo in range(STEP_LO)], axis=0)
    return jnp.dot(rowperm_t_ref[...], perm, preferred_element_type=F32)


def _inproj_body(x_ref, g_ref, wqku_ref, wvt_ref, rowperm_ref, laneperm_ref,
                 q_ref, k_ref, vt_ref, u_ref, xc_ref):
    n = _rms(x_ref[...], g_ref[...], NORM_EPS).astype(BF16)
    proj = jnp.dot(n, wqku_ref[...], preferred_element_type=F32)
    a = ATTN_WIDTH
    q_ref[...] = (proj[:, :a] * (LOG2E * HEAD_DIM ** -0.5)).astype(BF16)
    k_ref[...] = proj[:, a:2 * a].astype(BF16)
    u = proj[:, 2 * a:]
    u_ref[...] = u
    _to_chunk_layout(u, rowperm_ref, laneperm_ref, xc_ref)
    vt_ref[...] = lax.dot_general(wvt_ref[...], n, (((1,), (1,)), ((), ())),
                                  preferred_element_type=F32).astype(BF16)


def _inproj(x2, g, w_qku, w_vt, rowperm, laneperm):
    n_tok = x2.shape[0]
    tm = RELAYOUT_ROWS
    row = lambda w: pl.BlockSpec((tm, w), lambda i: (i, 0))
    full = lambda x: pl.BlockSpec(x.shape, lambda i: (0, 0))
    chunk_block = pl.BlockSpec((N_GROUP_PAIRS, CHUNKS_PER_TILE, 2 * CHUNK_W), lambda i: (0, i, 0))
    return pl.pallas_call(
        _inproj_body,
        grid=(n_tok // tm,),
        in_specs=[row(D_MODEL), full(g), full(w_qku), full(w_vt), full(rowperm), full(laneperm)],
        out_specs=[row(ATTN_WIDTH), row(ATTN_WIDTH),
                   pl.BlockSpec((ATTN_WIDTH, tm), lambda i: (0, i)), row(SSM_WIDTH), chunk_block],
        out_shape=[jax.ShapeDtypeStruct((n_tok, ATTN_WIDTH), BF16),
                   jax.ShapeDtypeStruct((n_tok, ATTN_WIDTH), BF16),
                   jax.ShapeDtypeStruct((ATTN_WIDTH, n_tok), BF16),
                   jax.ShapeDtypeStruct((n_tok, SSM_WIDTH), F32),
                   jax.ShapeDtypeStruct((N_GROUP_PAIRS, n_tok // SSM_CHUNK, 2 * CHUNK_W), BF16)],
        compiler_params=_params(("parallel",)),
        name="inproj",
    )(x2, g, w_qku, w_vt, rowperm, laneperm)


def _max_half_norm_sq(x):
    sq = x.astype(F32)
    sq = sq * sq
    lane = lax.broadcasted_iota(jnp.int32, sq.shape, 1)
    first = jnp.sum(jnp.where(lane < HEAD_DIM, sq, 0.0), axis=1, keepdims=True)
    second = jnp.sum(jnp.where(lane < HEAD_DIM, 0.0, sq), axis=1, keepdims=True)
    return jnp.max(jnp.maximum(first, second), axis=0, keepdims=True)


def _attn_body(q_ref, k_ref, vt_ref, lq1_ref, lk1_ref, lq2_ref, lk2_ref, g_ref, o_ref,
               qa_ref, kaug_ref, s_ref, p_ref, mx_ref, beta_ref, alpha_ref, m_ref, acc_ref,
               knorm_ref, *, tq, tk, nk, windowed):
    h = pl.program_id(1)
    i = pl.program_id(2)
    tq2 = 2 * tq
    sigma = jnp.float32(0.0)
    for head in range(N_HEADS):
        sigma = jnp.where(h == head,
                          jnp.float32(LOG2E * 2.0 ** (-8.0 * (head + 1) / N_HEADS)), sigma)
    q0 = i * tq
    jd = q0 // tk

    q = q_ref[0]
    lane = lax.broadcasted_iota(jnp.int32, q.shape, 1)
    zero = jnp.zeros_like(q)
    ones3 = jnp.where(lane < N_BIAS_LANES, 1.0, 0.0).astype(BF16)
    qa_ref[:tq, :HEAD_WIDTH] = jnp.where(lane < HEAD_DIM, q, zero)
    qa_ref[tq:, :HEAD_WIDTH] = jnp.where(lane < HEAD_DIM, zero, q)
    qa_ref[:tq, HEAD_WIDTH:] = ones3
    qa_ref[tq:, HEAD_WIDTH:] = ones3

    @pl.when(i == 0)
    def _():
        koff = lax.broadcasted_iota(jnp.int32, (tk, HEAD_WIDTH), 0).astype(F32) * sigma
        klane = lax.broadcasted_iota(jnp.int32, (tk, HEAD_WIDTH), 1)
        hi = koff.astype(BF16).astype(F32)
        mid = (koff - hi).astype(BF16).astype(F32)
        lo = koff - hi - mid
        ktile = jnp.where(klane == 0, hi,
                          jnp.where(klane == 1, mid, jnp.where(klane == 2, lo, 0.0)))
        kaug_ref[0] = ktile.astype(BF16)
        kaug_ref[1] = (-ktile).astype(BF16)

        if windowed:
            def knorm(c, best):
                rows = k_ref[0, pl.ds(pl.multiple_of(c * tk, tk), tk), :]
                return jnp.maximum(best, _max_half_norm_sq(rows))

            knorm_ref[...] = lax.fori_loop(0, nk, knorm, jnp.zeros((1, 1), F32))

    if windowed:
        qk = jnp.sqrt(_max_half_norm_sq(q) * knorm_ref[...]) * ATTN_NORM_SLACK
        reach = (2.0 * qk + ATTN_UNDERFLOW_LOG2) / sigma
        radius = jnp.maximum(jnp.floor((reach - 1.0) / tk) + 1.0, 0.0)
        radius = jnp.minimum(radius, float(nk)).astype(jnp.int32)[0, 0]
        jlo = jnp.maximum(jd - radius, 0)
        jhi = jnp.minimum(jd + radius, nk - 1)
        odd = lax.rem(jhi - jlo + 1, 2) == 1
        grow_hi = jnp.logical_and(odd, jhi < nk - 1)
        grow_lo = jnp.logical_and(odd, jhi >= nk - 1)
        jhi = jnp.where(grow_hi, jhi + 1, jhi)
        jlo = jnp.where(grow_lo, jlo - 1, jlo)
        nb = jhi - jlo + 1
    else:
        jlo, nb = 0, nk

    ql = lax.broadcasted_iota(jnp.int32, (1, tq2), 1)
    qpos = (q0 + jnp.where(ql < tq, ql, ql - tq)).astype(F32)
    ones_rows = jnp.where(
        lax.broadcasted_iota(jnp.int32, (V7X_BF16_ROWS, tk), 0) == 0, 1.0, 0.0).astype(BF16)

    def block_of(n):
        j = jlo + n - 1
        return jnp.where(n == 0, jd, jnp.where(j < jd, j, j + 1))

    lane_chunks = [pl.ds(c * V7X_MXU_WIDTH, V7X_MXU_WIDTH) for c in range(tq2 // V7X_MXU_WIDTH)]
    kk = lax.broadcasted_iota(jnp.int32, (tk, V7X_MXU_WIDTH), 0)
    qq = lax.broadcasted_iota(jnp.int32, (tk, V7X_MXU_WIDTH), 1)

    def keys_of(n):
        j = block_of(n)
        right = (j > jd).astype(jnp.int32)
        start = pl.multiple_of(j * tk, tk)
        ka = jnp.concatenate([k_ref[0, pl.ds(start, tk), :], kaug_ref[right]], axis=1)
        sgn = jnp.where(j > jd, jnp.float32(-1.0), jnp.float32(1.0))
        beta = (((j * tk).astype(F32) - qpos) * sigma) * sgn
        return ka, beta

    def scores_chunk(ka, beta, slot, c, diagonal=False):
        cs = lane_chunks[c]
        st = lax.dot_general(ka, qa_ref[cs, :], (((1,), (1,)), ((), ())),
                             preferred_element_type=F32)
        if diagonal:
            qoff = (c * V7X_MXU_WIDTH) % tq
            st = st + (jnp.maximum(kk - qq + (jd * tk - q0 - qoff), 0).astype(F32)
                       * sigma) * -2.0
        s_ref[slot, :, cs] = st
        mx_ref[slot, :, cs] = jnp.max(st, axis=0, keepdims=True) + beta[:, c * V7X_MXU_WIDTH:
                                                                      (c + 1) * V7X_MXU_WIDTH]

    def softmax_chunk(slot, c):
        cs = lane_chunks[c]
        m_prev = m_ref[:, cs]
        m_new = jnp.maximum(m_prev, mx_ref[slot, :, cs])
        alpha_ref[slot, :, cs] = jnp.exp2(m_prev - m_new)
        p_ref[slot, :, cs] = jnp.exp2(s_ref[slot, :, cs]
                                      - (m_new - beta_ref[slot, :, cs])).astype(BF16)
        m_ref[:, cs] = m_new

    def pv_chunk(vt, slot, c):
        cs = lane_chunks[c]
        acc_ref[:, cs] = alpha_ref[slot, :, cs] * acc_ref[:, cs] + jnp.dot(
            vt, p_ref[slot, :, cs], preferred_element_type=F32)

    def values_of(n):
        start = pl.multiple_of(block_of(n) * tk, tk)
        return jnp.concatenate([vt_ref[:, pl.ds(start, tk)], ones_rows], axis=0)

    def pipeline_step(n_scores, n_softmax, n_pv, diagonal=False):
        if n_scores is not None:
            ka, beta = keys_of(n_scores[0])
            beta_ref[n_scores[1]] = beta
        if n_pv is not None:
            vt = values_of(n_pv[0])
        for c in range(len(lane_chunks)):
            if n_softmax is not None:
                softmax_chunk(n_softmax, c)
            if n_scores is not None:
                scores_chunk(ka, beta, n_scores[1], c, diagonal)
            if n_pv is not None:
                pv_chunk(vt, n_pv[1], c)

    m_ref[...] = jnp.full(m_ref.shape, -jnp.inf, F32)
    acc_ref[...] = jnp.zeros(acc_ref.shape, F32)
    pipeline_step((0, 0), None, None, diagonal=True)
    pipeline_step((1, 1), 0, None)

    def pair(t, carry):
        n = 2 * t
        pipeline_step((n + 2, 0), 1, (n, 0))
        pipeline_step((n + 3, 1), 0, (n + 1, 1))
        return carry

    if windowed:
        n_pairs = lax.div(nb, 2) - 1
        lax.fori_loop(0, lax.div(n_pairs, 2), lambda t, c: pair(2 * t + 1, pair(2 * t, c)), 0)

        @pl.when(lax.rem(n_pairs, 2) == 1)
        def _():
            pair(n_pairs - 1, 0)
    else:
        n_pairs = nk // 2 - 1
        lax.fori_loop(0, n_pairs // 2, lambda t, c: pair(2 * t + 1, pair(2 * t, c)), 0)
        if n_pairs % 2:
            pair(n_pairs - 1, 0)
    pipeline_step(None, 1, (nb - 2, 0))
    pipeline_step(None, None, (nb - 1, 1))

    lam = (jnp.exp(jnp.sum(lq1_ref[...] * lk1_ref[...], axis=-1, keepdims=True))
           - jnp.exp(jnp.sum(lq2_ref[...] * lk2_ref[...], axis=-1, keepdims=True))
           + LAM_INIT)
    acc = acc_ref[...]
    o = acc[:HEAD_WIDTH, :] * (1.0 / acc[HEAD_WIDTH:HEAD_WIDTH + 1, :])
    at = o[:, :tq] - lam * o[:, tq:]
    ms = jnp.mean(at * at, axis=0, keepdims=True)
    at = at * lax.rsqrt(ms + SUBLN_EPS) * g_ref[...] * (1.0 - LAM_INIT)
    o_ref[0] = at.T.astype(o_ref.dtype)


def _attention(q, k, vt, lq1, lk1, lq2, lk2, g_subln_col, tq, tk):
    b, s, _ = q.shape
    nk = s // tk
    assert tk % tq == 0 and nk % 2 == 0 and nk >= 2
    vec = lambda n: pl.BlockSpec((1, n), lambda bi, hi, qi: (0, 0))
    return pl.pallas_call(
        functools.partial(_attn_body, tq=tq, tk=tk, nk=nk,
                          windowed=nk > ATTN_WINDOW_MIN_BLOCKS),
        grid=(b, N_HEADS, s // tq),
        in_specs=[pl.BlockSpec((1, tq, HEAD_WIDTH), lambda bi, hi, qi: (bi, qi, hi)),
                  pl.BlockSpec((1, s, HEAD_WIDTH), lambda bi, hi, qi: (bi, 0, hi)),
                  pl.BlockSpec((HEAD_WIDTH, s), lambda bi, hi, qi: (hi, bi)),
                  vec(HEAD_DIM), vec(HEAD_DIM), vec(HEAD_DIM), vec(HEAD_DIM),
                  pl.BlockSpec((HEAD_WIDTH, 1), lambda bi, hi, qi: (0, 0))],
        out_specs=pl.BlockSpec((1, tq, HEAD_WIDTH), lambda bi, hi, qi: (bi, qi, hi)),
        out_shape=jax.ShapeDtypeStruct((b, s, ATTN_WIDTH), BF16),
        scratch_shapes=[pltpu.VMEM((2 * tq, 2 * HEAD_WIDTH), BF16),
                        pltpu.VMEM((2, tk, HEAD_WIDTH), BF16),
                        pltpu.VMEM((2, tk, 2 * tq), F32),
                        pltpu.VMEM((2, tk, 2 * tq), BF16),
                        pltpu.VMEM((2, 1, 2 * tq), F32),
                        pltpu.VMEM((2, 1, 2 * tq), F32),
                        pltpu.VMEM((2, 1, 2 * tq), F32),
                        pltpu.VMEM((1, 2 * tq), F32),
                        pltpu.VMEM((HEAD_WIDTH + V7X_BF16_ROWS, 2 * tq), F32),
                        pltpu.VMEM((1, 1), F32)],
        compiler_params=_params(("parallel", "parallel", "arbitrary")),
        name="diff_attention",
    )(q, k, vt, lq1, lk1, lq2, lk2, g_subln_col)


def _cpow(lre, lim, n):
    shape = jnp.broadcast_shapes(lre.shape, n.shape)
    pre = jnp.ones(shape, F32)
    pim = jnp.zeros(shape, F32)
    bre, bim = lre, lim
    for bit in range(POW_BITS):
        on = ((n >> bit) & 1) == 1
        mre = jnp.where(on, bre, 1.0)
        mim = jnp.where(on, bim, 0.0)
        pre, pim = pre * mre - pim * mim, pre * mim + pim * mre
        bre, bim = bre * bre - bim * bim, 2.0 * bre * bim
    return pre, pim


def _discretise(a_re, a_im, log_step):
    step = jnp.exp(log_step)
    mag = jnp.exp(a_re * step)
    lre = mag * jnp.cos(a_im * step)
    lim = mag * jnp.sin(a_im * step)
    den = a_re * a_re + a_im * a_im
    nr = lre - 1.0
    fre = (nr * a_re + lim * a_im) / den
    fim = (lim * a_re - nr * a_im) / den
    return lre, lim, fre, fim


def _ssm_weights_body(arow_re, arow_im, acol_re, acol_im, lsrow, lscol, bt_re, bt_im,
                      ct_re, ct_im, wt_ref, ws_ref, wo_ref, at_ref):
    t, cw = SSM_CHUNK, CHUNK_W
    lane_pair = lax.broadcasted_iota(jnp.int32, (1, PAIR_STATE), 1) // STATE
    row_pair = lax.broadcasted_iota(jnp.int32, (PAIR_STATE, 1), 0) // STATE

    ch = lax.broadcasted_iota(jnp.int32, (SSM_GROUP, LAG_W), 0)
    ln = lax.broadcasted_iota(jnp.int32, (SSM_GROUP, LAG_W), 1)
    tile_lag = jnp.where((ln % SSM_GROUP) == ch, 1.0, 0.0).astype(F32)

    step_rows = lax.broadcasted_iota(jnp.int32, (cw, 1), 0) // SSM_GROUP
    slot = lax.broadcasted_iota(jnp.int32, (1, LAG_W), 1) // SSM_GROUP
    step_lanes = lax.broadcasted_iota(jnp.int32, (1, cw), 1) // SSM_GROUP

    zt = [jnp.zeros((SSM_GROUP, LAG_W), F32) for _ in range(2)]
    for d in range(2):
        lre, lim, fre, fim = _discretise(arow_re[d, 0], arow_im[d, 0], lsrow[d, 0])
        bre, bim = bt_re[d, 0], bt_im[d, 0]
        bbre = fre * bre - fim * bim
        bbim = fre * bim + fim * bre
        expo = (t - 1 - step_rows) if d == 0 else step_rows
        pre, pim = _cpow(lre, lim, expo)
        tbre = jnp.tile(bbre, (t, 1))
        tbim = jnp.tile(bbim, (t, 1))
        inj_re = pre * tbre - pim * tbim
        inj_im = pre * tbim + pim * tbre
        for g2 in range(2):
            keep = lane_pair == g2
            rows = pl.ds(g2 * cw, cw)
            ws_ref[0, rows, pl.ds((2 * d) * PAIR_STATE, PAIR_STATE)] = (
                jnp.where(keep, inj_re, 0.0).astype(ws_ref.dtype))
            ws_ref[0, rows, pl.ds((2 * d + 1) * PAIR_STATE, PAIR_STATE)] = (
                jnp.where(keep, inj_im, 0.0).astype(ws_ref.dtype))
        dre, dim_ = _cpow(lre, lim, jnp.full((1, 1), t, jnp.int32))
        at_ref[0, pl.ds(2 * d, 1), :] = dre
        at_ref[0, pl.ds(2 * d + 1, 1), :] = dim_

        cre, cim, _, _ = _discretise(acol_re[d, 0], acol_im[d, 0], lscol[d, 0])
        ctl_re = jnp.dot(ct_re[d, 0], tile_lag, precision=HI, preferred_element_type=F32)
        ctl_im = jnp.dot(ct_im[d, 0], tile_lag, precision=HI, preferred_element_type=F32)
        lag = (slot - (t - 1)) if d == 0 else ((t - 1) - slot)
        valid = (lag >= 0) & (slot < 2 * t - 1)
        qre, qim = _cpow(cre, cim, jnp.maximum(lag, 0))
        r_re = jnp.where(valid, qre * ctl_re - qim * ctl_im, 0.0)
        r_im = jnp.where(valid, qre * ctl_im + qim * ctl_re, 0.0)
        for g2 in range(2):
            keep = lane_pair == g2
            zt[g2] = (zt[g2]
                      + jnp.dot(jnp.where(keep, bbre, 0.0), r_re, precision=HI,
                                preferred_element_type=F32)
                      - jnp.dot(jnp.where(keep, bbim, 0.0), r_im, precision=HI,
                                preferred_element_type=F32))

        out_pow = (step_lanes + 1) if d == 0 else (t - step_lanes)
        ore, oim = _cpow(cre, cim, out_pow)
        c_re = ctl_re[:, :cw]
        c_im = ctl_im[:, :cw]
        rd_re = ore * c_re - oim * c_im
        rd_im = -(ore * c_im + oim * c_re)
        for g2 in range(2):
            keep = row_pair == g2
            cols = pl.ds(g2 * cw, cw)
            wo_ref[0, pl.ds((2 * d) * PAIR_STATE, PAIR_STATE), cols] = (
                jnp.where(keep, rd_re, 0.0).astype(wo_ref.dtype))
            wo_ref[0, pl.ds((2 * d + 1) * PAIR_STATE, PAIR_STATE), cols] = (
                jnp.where(keep, rd_im, 0.0).astype(wo_ref.dtype))

    for g2 in range(2):
        for s in range(t):
            off = (t - 1 - s) * SSM_GROUP
            shifted = zt[g2] if off == 0 else pltpu.roll(zt[g2], LAG_W - off, 1)
            wt_ref[0, g2, pl.ds(s * SSM_GROUP, SSM_GROUP), :] = (
                shifted[:, :cw].astype(wt_ref.dtype))


def _ssm_weights(a_re, a_im, log_step, b_re, b_im, c_re, c_im):
    np_, g2p = N_GROUP_PAIRS, PAIR_STATE
    arow = lambda a: a.reshape(2, np_, 1, g2p)
    acol = lambda a: a.reshape(2, np_, g2p, 1)
    ls = jnp.repeat(log_step, STATE, axis=-1)
    bt = lambda w: (w.reshape(2, np_, 2, STATE, SSM_GROUP)
                    .transpose(0, 1, 4, 2, 3).reshape(2, np_, SSM_GROUP, g2p))
    ct = lambda w: (w.reshape(2, np_, 2, SSM_GROUP, STATE)
                    .transpose(0, 1, 2, 4, 3).reshape(2, np_, g2p, SSM_GROUP))
    ins = [arow(a_re), arow(a_im), acol(a_re), acol(a_im), arow(ls), acol(ls),
           bt(b_re), bt(b_im), ct(c_re), ct(c_im)]
    spec = lambda x: pl.BlockSpec((2, 1) + x.shape[2:], lambda k: (0, k, 0, 0))
    return pl.pallas_call(
        _ssm_weights_body,
        grid=(np_,),
        in_specs=[spec(x) for x in ins],
        out_specs=[pl.BlockSpec((1, 2, CHUNK_W, CHUNK_W), lambda k: (k, 0, 0, 0)),
                   pl.BlockSpec((1, 2 * CHUNK_W, N_SLABS * g2p), lambda k: (k, 0, 0)),
                   pl.BlockSpec((1, N_SLABS * g2p, 2 * CHUNK_W), lambda k: (k, 0, 0)),
                   pl.BlockSpec((1, N_SLABS, g2p), lambda k: (k, 0, 0))],
        out_shape=[jax.ShapeDtypeStruct((np_, 2, CHUNK_W, CHUNK_W), BF16),
                   jax.ShapeDtypeStruct((np_, 2 * CHUNK_W, N_SLABS * g2p), BF16),
                   jax.ShapeDtypeStruct((np_, N_SLABS * g2p, 2 * CHUNK_W), BF16),
                   jax.ShapeDtypeStruct((np_, N_SLABS, g2p), F32)],
        compiler_params=_params(("parallel",)),
        name="ssm_weights",
    )(*ins)


def _ssm_inject_body(x_ref, ws_ref, s_ref):
    s = jnp.dot(x_ref[0], ws_ref[0], preferred_element_type=F32)
    for slab in range(N_SLABS):
        s_ref[slab] = s[:, slab * PAIR_STATE:(slab + 1) * PAIR_STATE]


def _ssm_inject(x, ws):
    np_, r, _ = x.shape
    return pl.pallas_call(
        _ssm_inject_body,
        grid=(np_,),
        in_specs=[pl.BlockSpec((1, r, 2 * CHUNK_W), lambda k: (k, 0, 0)),
                  pl.BlockSpec((1,) + ws.shape[1:], lambda k: (k, 0, 0))],
        out_specs=pl.BlockSpec((N_SLABS, r, PAIR_STATE), lambda k: (0, 0, k)),
        out_shape=jax.ShapeDtypeStruct((N_SLABS, r, np_ * PAIR_STATE), F32),
        compiler_params=_params(("parallel",)),
        name="ssm_inject",
    )(x, ws)


def _ssm_scan_body(s_ref, at_ref, hin_ref, *, tiles_per_seq, n_tiles):
    sub = V7X_SUBLANES
    lanes = s_ref.shape[-1]
    a = [at_ref[pl.ds(slab, 1), :] for slab in range(N_SLABS)]

    def sweep(d, tile, carry):
        edge = 0 if d == 0 else tiles_per_seq - 1
        keep = jnp.where(lax.rem(tile, tiles_per_seq) == edge,
                         jnp.float32(0.0), jnp.float32(1.0))
        hre, him = carry[0] * keep, carry[1] * keep
        are, aim = a[2 * d], a[2 * d + 1]
        row0 = pl.multiple_of(tile * V7X_SUBLANES, V7X_SUBLANES)
        sre = s_ref[2 * d, pl.ds(row0, V7X_SUBLANES), :]
        sim = s_ref[2 * d + 1, pl.ds(row0, V7X_SUBLANES), :]
        out_re = [None] * sub
        out_im = [None] * sub
        order = range(sub) if d == 0 else range(sub - 1, -1, -1)
        for q in order:
            out_re[q], out_im[q] = hre, him
            xr = sre[q:q + 1]
            xi = sim[q:q + 1]
            hre, him = are * hre - aim * him + xr, are * him + aim * hre + xi
        hin_ref[2 * d, pl.ds(row0, V7X_SUBLANES), :] = jnp.concatenate(out_re, axis=0)
        hin_ref[2 * d + 1, pl.ds(row0, V7X_SUBLANES), :] = jnp.concatenate(out_im, axis=0)
        return hre, him

    zero = (jnp.zeros((1, lanes), F32), jnp.zeros((1, lanes), F32))
    lax.fori_loop(0, n_tiles, lambda i, c: sweep(0, i, c), zero)
    lax.fori_loop(0, n_tiles, lambda i, c: sweep(1, n_tiles - 1 - i, c), zero)


def _ssm_scan(s, at, chunks_per_seq, lane_block):
    _, r, lanes = s.shape
    assert chunks_per_seq % V7X_SUBLANES == 0 and r % chunks_per_seq == 0
    return pl.pallas_call(
        functools.partial(_ssm_scan_body, tiles_per_seq=chunks_per_seq // V7X_SUBLANES,
                          n_tiles=r // V7X_SUBLANES),
        grid=(lanes // lane_block,),
        in_specs=[pl.BlockSpec((N_SLABS, r, lane_block), lambda k: (0, 0, k)),
                  pl.BlockSpec((N_SLABS, lane_block), lambda k: (0, k))],
        out_specs=pl.BlockSpec((N_SLABS, r, lane_block), lambda k: (0, 0, k)),
        out_shape=jax.ShapeDtypeStruct(s.shape, F32),
        compiler_params=_params(("parallel",)),
        name="ssm_scan",
    )(s, at)


def _ssm_mix_body(x_ref, wt_ref, hin_ref, wo_ref, y_ref):
    x = x_ref[0]
    hin = jnp.concatenate([hin_ref[slab] for slab in range(N_SLABS)], axis=1).astype(BF16)
    carried = jnp.dot(hin, wo_ref[0], preferred_element_type=F32)
    for g2 in range(2):
        cols = slice(g2 * CHUNK_W, (g2 + 1) * CHUNK_W)
        y_ref[0, :, cols] = (carried[:, cols] + jnp.dot(
            x[:, cols], wt_ref[0, g2], preferred_element_type=F32)).astype(y_ref.dtype)


def _ssm_mix(x, wt, hin, wo):
    np_, r, _ = x.shape
    return pl.pallas_call(
        _ssm_mix_body,
        grid=(np_,),
        in_specs=[pl.BlockSpec((1, r, 2 * CHUNK_W), lambda k: (k, 0, 0)),
                  pl.BlockSpec((1,) + wt.shape[1:], lambda k: (k, 0, 0, 0)),
                  pl.BlockSpec((N_SLABS, r, PAIR_STATE), lambda k: (0, 0, k)),
                  pl.BlockSpec((1,) + wo.shape[1:], lambda k: (k, 0, 0))],
        out_specs=pl.BlockSpec((1, r, 2 * CHUNK_W), lambda k: (k, 0, 0)),
        out_shape=jax.ShapeDtypeStruct((np_, r, 2 * CHUNK_W), BF16),
        compiler_params=_params(("parallel",)),
        name="ssm_mix",
    )(x, wt, hin, wo)


def _outproj_body(x_ref, a_ref, yc_ref, u_ref, d_ref, wg_ref, bg_ref, gs_ref, wo_ref, gf_ref,
                  laneperm_t_ref, rowperm_t_ref, x1_ref, h_ref):
    y_ssm = _from_chunk_layout(yc_ref, laneperm_t_ref, rowperm_t_ref)
    y = jax.nn.gelu(y_ssm + d_ref[...] * u_ref[...])
    gate = jnp.dot(y.astype(BF16), wg_ref[...], preferred_element_type=F32) + bg_ref[...]
    s = _rms(y * jax.nn.sigmoid(gate), gs_ref[...], NORM_EPS)
    mixed = (jnp.dot(a_ref[...], wo_ref[:ATTN_WIDTH, :], preferred_element_type=F32)
             + jnp.dot(s.astype(BF16), wo_ref[ATTN_WIDTH:, :], preferred_element_type=F32))
    x1 = x_ref[...] + mixed
    x1_ref[...] = x1
    h_ref[...] = _rms(x1, gf_ref[...], NORM_EPS).astype(h_ref.dtype)


def _outproj(x2, a2, yc, u2, d, w_glu, b_glu, g_ssm, w_out, g_ffn, laneperm_t, rowperm_t):
    n_tok = x2.shape[0]
    tm = RELAYOUT_ROWS
    row = lambda w: pl.BlockSpec((tm, w), lambda i: (i, 0))
    full = lambda x: pl.BlockSpec(x.shape, lambda i: (0, 0))
    chunk_block = pl.BlockSpec((N_GROUP_PAIRS, CHUNKS_PER_TILE, 2 * CHUNK_W), lambda i: (0, i, 0))
    return pl.pallas_call(
        _outproj_body,
        grid=(n_tok // tm,),
        in_specs=[row(D_MODEL), row(ATTN_WIDTH), chunk_block, row(SSM_WIDTH),
                  full(d), full(w_glu), full(b_glu), full(g_ssm), full(w_out), full(g_ffn),
                  full(laneperm_t), full(rowperm_t)],
        out_specs=[row(D_MODEL), row(D_MODEL)],
        out_shape=[jax.ShapeDtypeStruct((n_tok, D_MODEL), F32),
                   jax.ShapeDtypeStruct((n_tok, D_MODEL), BF16)],
        compiler_params=_params(("parallel",)),
        name="outproj",
    )(x2, a2, yc, u2, d, w_glu, b_glu, g_ssm, w_out, g_ffn, laneperm_t, rowperm_t)


def _ffn_body(h_ref, hp_ref, hn_ref, x1_ref, wu_ref, cw_ref, cb_ref, wd_ref, gf_ref, o_ref,
              hcat_ref, *, tm, tiles_per_seq, ff_chunk):
    i = pl.program_id(0)
    halo = V7X_BF16_ROWS
    first = (i % tiles_per_seq) == 0
    last = (i % tiles_per_seq) == tiles_per_seq - 1
    zeros = jnp.zeros((halo, D_MODEL), hcat_ref.dtype)
    hcat_ref[:halo, :] = jnp.where(first, zeros, hp_ref[...])
    hcat_ref[halo:halo + tm, :] = h_ref[...]
    hcat_ref[halo + tm:, :] = jnp.where(last, zeros, hn_ref[...])
    hcat = hcat_ref[...]
    rows = tm + 2 * halo

    def conv(z, col0):
        cols = pl.ds(col0, ff_chunk)
        prev = pltpu.roll(z, 1, 0)[halo:halo + tm]
        nxt = pltpu.roll(z, rows - 1, 0)[halo:halo + tm]
        return (prev * cw_ref[pl.ds(0, 1), cols] + z[halo:halo + tm] * cw_ref[pl.ds(1, 1), cols]
                + nxt * cw_ref[pl.ds(2, 1), cols] + cb_ref[:, cols])

    def up(c):
        g0 = c * ff_chunk
        return (jnp.dot(hcat, wu_ref[:, pl.ds(g0, ff_chunk)], preferred_element_type=F32),
                jnp.dot(hcat, wu_ref[:, pl.ds(D_FF + g0, ff_chunk)],
                        preferred_element_type=F32))

    n_chunks = D_FF // ff_chunk
    acc = jnp.zeros((tm, D_MODEL), F32)
    z_next = up(0)
    for c in range(n_chunks):
        zg, zv = z_next
        if c + 1 < n_chunks:
            z_next = up(c + 1)
        g0 = c * ff_chunk
        act = (jax.nn.gelu(conv(zg, g0)) * conv(zv, D_FF + g0)).astype(BF16)
        acc = acc + jnp.dot(act, wd_ref[pl.ds(g0, ff_chunk), :], preferred_element_type=F32)
    o_ref[...] = _rms(x1_ref[...] + acc, gf_ref[...], NORM_EPS)


def _ffn(h2, x1, w_up, conv_w, conv_b, w_down, g_final, tm, seq, ff_chunk):
    n_tok = h2.shape[0]
    halo = V7X_BF16_ROWS
    per = tm // halo
    n_halo = n_tok // halo
    row = lambda w: pl.BlockSpec((tm, w), lambda i: (i, 0))
    once = lambda x: pl.BlockSpec(x.shape, lambda i: (0, 0), pipeline_mode=pl.Buffered(1))
    return pl.pallas_call(
        functools.partial(_ffn_body, tm=tm, tiles_per_seq=seq // tm, ff_chunk=ff_chunk),
        grid=(n_tok // tm,),
        in_specs=[row(D_MODEL),
                  pl.BlockSpec((halo, D_MODEL), lambda i: (jnp.maximum(i * per - 1, 0), 0)),
                  pl.BlockSpec((halo, D_MODEL),
                               lambda i: (jnp.minimum((i + 1) * per, n_halo - 1), 0)),
                  row(D_MODEL), once(w_up), once(conv_w), once(conv_b), once(w_down),
                  once(g_final)],
        out_specs=row(D_MODEL),
        out_shape=jax.ShapeDtypeStruct((n_tok, D_MODEL), F32),
        scratch_shapes=[pltpu.VMEM((tm + 2 * halo, D_MODEL), BF16)],
        compiler_params=_params(("parallel",)),
        name="ffn",
    )(h2, h2, h2, x1, w_up, conv_w, conv_b, w_down, g_final)


def _tile(pref, n):
    t = min(pref, n)
    assert n % t == 0, (pref, n)
    return t


def _trunk(x, p, ssm_w, perms):
    batch, seq, _ = x.shape
    n_tok = batch * seq
    x2 = x.reshape(n_tok, D_MODEL)
    assert seq % RELAYOUT_ROWS == 0
    rowperm, laneperm, laneperm_t, rowperm_t = perms

    q, k, vt, u, xc = _inproj(x2, p["g_mix"], p["w_qku"], p["w_vt"], rowperm, laneperm)

    shape3 = (batch, seq, ATTN_WIDTH)
    a = _attention(q.reshape(shape3), k.reshape(shape3), vt,
                   p["lq1"], p["lk1"], p["lq2"], p["lk2"], p["g_subln"],
                   _tile(512, seq), _tile(512, seq))

    wt, ws, wo, at = ssm_w
    s_in = _ssm_inject(xc, ws)
    at_flat = at.transpose(1, 0, 2).reshape(N_SLABS, N_GROUP_PAIRS * PAIR_STATE)
    hin = _ssm_scan(s_in, at_flat, seq // SSM_CHUNK, 4 * PAIR_STATE)
    yc = _ssm_mix(xc, wt, hin, wo)

    x1, h = _outproj(x2, a.reshape(n_tok, ATTN_WIDTH), yc, u, p["d"], p["w_glu"], p["b_glu"],
                     p["g_ssm"], p["w_out"], p["g_ffn"], laneperm_t, rowperm_t)
    out = _ffn(h, x1, p["w_up"], p["conv_w"], p["conv_b"], p["w_down"], p["g_final"],
               _tile(512, seq), seq, 1408)
    return out.reshape(batch, seq, D_MODEL)


def kernel(x_prompt, x_sample, g_mix_norm, w_in, lambda_q1, lambda_k1, lambda_q2, lambda_k2, g_subln, ssm_a_re, ssm_a_im, ssm_log_step, ssm_b_re, ssm_b_im, ssm_c_re, ssm_c_im, ssm_d, w_glu, b_glu, g_ssm_out, w_out, g_ffn_norm, w_up, conv_w, conv_b, w_down, g_final):
    layer = 0
    vec = lambda t: t.reshape(1, -1).astype(F32)
    p = {
        "g_mix": vec(g_mix_norm[layer]),
        "w_qku": jnp.concatenate([w_in[layer][:, :2 * ATTN_WIDTH],
                                  w_in[layer][:, 3 * ATTN_WIDTH:]], axis=1).astype(BF16),
        "w_vt": w_in[layer][:, 2 * ATTN_WIDTH:3 * ATTN_WIDTH].T.astype(BF16),
        "lq1": vec(lambda_q1[layer]), "lk1": vec(lambda_k1[layer]),
        "lq2": vec(lambda_q2[layer]), "lk2": vec(lambda_k2[layer]),
        "g_subln": g_subln[layer].reshape(-1, 1).astype(F32), "d": vec(ssm_d[layer]),
        "w_glu": w_glu[layer].astype(BF16), "b_glu": vec(b_glu[layer]),
        "g_ssm": vec(g_ssm_out[layer]), "w_out": w_out[layer].astype(BF16),
        "g_ffn": vec(g_ffn_norm[layer]), "w_up": w_up[layer].astype(BF16),
        "conv_w": conv_w[layer].astype(F32), "conv_b": vec(conv_b[layer]),
        "w_down": w_down[layer].astype(BF16), "g_final": vec(g_final),
    }
    f32 = lambda t: t[layer].astype(F32)
    ssm_w = _ssm_weights(f32(ssm_a_re), f32(ssm_a_im), f32(ssm_log_step), f32(ssm_b_re),
                         f32(ssm_b_im), f32(ssm_c_re), f32(ssm_c_im))
    perms = _relayout_constants()
    return _trunk(x_prompt, p, ssm_w, perms), _trunk(x_sample, p, ssm_w, perms)
```

```python
import functools
import math

import jax
import jax.numpy as jnp
import numpy as np
from jax import lax
from jax.experimental import pallas as pl
from jax.experimental.pallas import tpu as pltpu

F32 = jnp.float32
BF16 = jnp.bfloat16

D_MODEL = 1024
ATTN_WIDTH = 512
SSM_WIDTH = 512
N_HEADS = 4
HEAD_DIM = 64
HEAD_WIDTH = 2 * HEAD_DIM
SSM_GROUP = 16
N_SSM_GROUPS = 32
N_GROUP_PAIRS = N_SSM_GROUPS // 2
STATE = 64
PAIR_STATE = 2 * STATE
D_FF = 2816
NORM_EPS = 1e-6
SUBLN_EPS = 1e-5
LAM_INIT = 0.8 - 0.6 * math.exp(-0.3 * 0)
LOG2E = math.log2(math.e)
N_BIAS_LANES = 3
ATTN_UNDERFLOW_LOG2 = 152.0
ATTN_NORM_SLACK = 1.001
ATTN_WINDOW_MIN_BLOCKS = 8
ATTN_Q_TILE = 512
ATTN_K_TILE = 512
FFN_ROW_TILE = 512
FFN_COL_CHUNK = D_FF // 2

SSM_CHUNK = 32
CHUNK_W = SSM_CHUNK * SSM_GROUP
LAG_W = 2 * CHUNK_W
POW_BITS = SSM_CHUNK.bit_length()
N_SLABS = 4

V7X_MXU_WIDTH = 256
V7X_SUBLANES = 8
V7X_BF16_ROWS = 16
V7X_VMEM_LIMIT = 56 * 1024 * 1024

HI = lax.Precision.HIGHEST


def _params(sem):
    return pltpu.CompilerParams(dimension_semantics=sem, vmem_limit_bytes=V7X_VMEM_LIMIT)


def _rms(x, g, eps):
    return x * lax.rsqrt(jnp.mean(x * x, axis=-1, keepdims=True) + eps) * g


RELAYOUT_ROWS = 512
CHUNKS_PER_TILE = RELAYOUT_ROWS // SSM_CHUNK
LANE = 128
STEP_LO = LANE // SSM_GROUP
STEP_HI = SSM_CHUNK // STEP_LO
LANE_TILES = SSM_WIDTH // LANE
PERM_W = STEP_LO * LANE


def _relayout_constants():
    rows = np.zeros((RELAYOUT_ROWS, RELAYOUT_ROWS), np.float32)
    for c in range(CHUNKS_PER_TILE):
        for s in range(SSM_CHUNK):
            rows[s * CHUNKS_PER_TILE + c, c * SSM_CHUNK + s] = 1.0
    lanes = np.zeros((PERM_W, PERM_W), np.float32)
    for s_lo in range(STEP_LO):
        for g8 in range(STEP_LO):
            for i in range(SSM_GROUP):
                lanes[s_lo * LANE + g8 * SSM_GROUP + i, g8 * LANE + s_lo * SSM_GROUP + i] = 1.0
    as_bf = lambda m: jnp.asarray(m, BF16)
    return as_bf(rows), as_bf(lanes), as_bf(lanes.T), as_bf(rows.T)


def _pair_lanes(lane_tile, g8, step_hi):
    group = lane_tile * STEP_LO + g8
    return group // 2, (group % 2) * CHUNK_W + step_hi * LANE


def _to_chunk_layout(u, rowperm_ref, laneperm_ref, x_ref):
    nct = CHUNKS_PER_TILE
    up = jnp.dot(rowperm_ref[...], u.astype(BF16), preferred_element_type=F32).astype(BF16)
    stacked = jnp.concatenate(
        [jnp.concatenate([up[(hi * STEP_LO + lo) * nct:(hi * STEP_LO + lo + 1) * nct,
                             lt * LANE:(lt + 1) * LANE] for lo in range(STEP_LO)], axis=1)
         for lt in range(LANE_TILES) for hi in range(STEP_HI)], axis=0)
    out = jnp.dot(stacked, laneperm_ref[...], preferred_element_type=F32).astype(BF16)
    for lt in range(LANE_TILES):
        for hi in range(STEP_HI):
            r0 = (lt * STEP_HI + hi) * nct
            for g8 in range(STEP_LO):
                pair, lane0 = _pair_lanes(lt, g8, hi)
                x_ref[pair, :, pl.ds(lane0, LANE)] = out[r0:r0 + nct, g8 * LANE:(g8 + 1) * LANE]


def _from_chunk_layout(y_ref, laneperm_t_ref, rowperm_t_ref):
    nct = CHUNKS_PER_TILE
    pieces = []
    for lt in range(LANE_TILES):
        for hi in range(STEP_HI):
            row = []
            for g8 in range(STEP_LO):
                pair, lane0 = _pair_lanes(lt, g8, hi)
                row.append(y_ref[pair, :, pl.ds(lane0, LANE)])
            pieces.append(jnp.concatenate(row, axis=1))
    stacked = jnp.concatenate(pieces, axis=0)
    z = jnp.dot(stacked.astype(BF16), laneperm_t_ref[...],
                preferred_element_type=F32).astype(BF16)
    perm = jnp.concatenate(
        [jnp.concatenate([z[(lt * STEP_HI + hi) * nct:(lt * STEP_HI + hi + 1) * nct,
                            lo * LANE:(lo + 1) * LANE] for lt in range(LANE_TILES)], axis=1)
         for hi in range(STEP_HI) for lo in range(STEP_LO)], axis=0)
    return jnp.dot(rowperm_t_ref[...], perm, preferred_element_type=F32)


def _inproj_body(x_ref, g_ref, wqku_ref, wvt_ref, rowperm_ref, laneperm_ref,
                 q_ref, k_ref, vt_ref, u_ref, xc_ref):
    n = _rms(x_ref[...], g_ref[...], NORM_EPS).astype(BF16)
    proj = jnp.dot(n, wqku_ref[...], preferred_element_type=F32)
    a = ATTN_WIDTH
    q_ref[...] = (proj[:, :a] * (LOG2E * HEAD_DIM ** -0.5)).astype(BF16)
    k_ref[...] = proj[:, a:2 * a].astype(BF16)
    u = proj[:, 2 * a:]
    u_ref[...] = u
    _to_chunk_layout(u, rowperm_ref, laneperm_ref, xc_ref)
    vt_ref[...] = lax.dot_general(wvt_ref[...], n, (((1,), (1,)), ((), ())),
                                  preferred_element_type=F32).astype(BF16)


def _inproj(x2, g, w_qku, w_vt, rowperm, laneperm):
    n_tok = x2.shape[0]
    tm = RELAYOUT_ROWS
    row = lambda w: pl.BlockSpec((tm, w), lambda i: (i, 0))
    full = lambda x: pl.BlockSpec(x.shape, lambda i: (0, 0))
    chunk_block = pl.BlockSpec((N_GROUP_PAIRS, CHUNKS_PER_TILE, 2 * CHUNK_W), lambda i: (0, i, 0))
    return pl.pallas_call(
        _inproj_body,
        grid=(n_tok // tm,),
        in_specs=[row(D_MODEL), full(g), full(w_qku), full(w_vt), full(rowperm), full(laneperm)],
        out_specs=[row(ATTN_WIDTH), row(ATTN_WIDTH),
                   pl.BlockSpec((ATTN_WIDTH, tm), lambda i: (0, i)), row(SSM_WIDTH), chunk_block],
        out_shape=[jax.ShapeDtypeStruct((n_tok, ATTN_WIDTH), BF16),
                   jax.ShapeDtypeStruct((n_tok, ATTN_WIDTH), BF16),
                   jax.ShapeDtypeStruct((ATTN_WIDTH, n_tok), BF16),
                   jax.ShapeDtypeStruct((n_tok, SSM_WIDTH), F32),
                   jax.ShapeDtypeStruct((N_GROUP_PAIRS, n_tok // SSM_CHUNK, 2 * CHUNK_W), BF16)],
        compiler_params=_params(("parallel",)),
        name="inproj",
    )(x2, g, w_qku, w_vt, rowperm, laneperm)


def _max_half_norm_sq(x):
    sq = x.astype(F32)
    sq = sq * sq
    lane = lax.broadcasted_iota(jnp.int32, sq.shape, 1)
    first = jnp.sum(jnp.where(lane < HEAD_DIM, sq, 0.0), axis=1, keepdims=True)
    second = jnp.sum(jnp.where(lane < HEAD_DIM, 0.0, sq), axis=1, keepdims=True)
    return jnp.max(jnp.maximum(first, second), axis=0, keepdims=True)


def _attn_body(q_ref, k_ref, vt_ref, lq1_ref, lk1_ref, lq2_ref, lk2_ref, g_ref, o_ref,
               qa_ref, kaug_ref, s_ref, p_ref, mx_ref, beta_ref, alpha_ref, m_ref, acc_ref,
               knorm_ref, *, tq, tk, nk, windowed):
    h = pl.program_id(1)
    i = pl.program_id(2)
    tq2 = 2 * tq
    sigma = jnp.float32(0.0)
    for head in range(N_HEADS):
        sigma = jnp.where(h == head,
                          jnp.float32(LOG2E * 2.0 ** (-8.0 * (head + 1) / N_HEADS)), sigma)
    q0 = i * tq
    jd = q0 // tk

    q = q_ref[0]
    lane = lax.broadcasted_iota(jnp.int32, q.shape, 1)
    zero = jnp.zeros_like(q)
    ones3 = jnp.where(lane < N_BIAS_LANES, 1.0, 0.0).astype(BF16)
    qa_ref[:tq, :HEAD_WIDTH] = jnp.where(lane < HEAD_DIM, q, zero)
    qa_ref[tq:, :HEAD_WIDTH] = jnp.where(lane < HEAD_DIM, zero, q)
    qa_ref[:tq, HEAD_WIDTH:] = ones3
    qa_ref[tq:, HEAD_WIDTH:] = ones3

    @pl.when(i == 0)
    def _():
        koff = lax.broadcasted_iota(jnp.int32, (tk, HEAD_WIDTH), 0).astype(F32) * sigma
        klane = lax.broadcasted_iota(jnp.int32, (tk, HEAD_WIDTH), 1)
        hi = koff.astype(BF16).astype(F32)
        mid = (koff - hi).astype(BF16).astype(F32)
        lo = koff - hi - mid
        ktile = jnp.where(klane == 0, hi,
                          jnp.where(klane == 1, mid, jnp.where(klane == 2, lo, 0.0)))
        kaug_ref[0] = ktile.astype(BF16)
        kaug_ref[1] = (-ktile).astype(BF16)

        if windowed:
            def knorm(c, best):
                rows = k_ref[0, pl.ds(pl.multiple_of(c * tk, tk), tk), :]
                return jnp.maximum(best, _max_half_norm_sq(rows))

            knorm_ref[...] = lax.fori_loop(0, nk, knorm, jnp.zeros((1, 1), F32))

    if windowed:
        qk = jnp.sqrt(_max_half_norm_sq(q) * knorm_ref[...]) * ATTN_NORM_SLACK
        reach = (2.0 * qk + ATTN_UNDERFLOW_LOG2) / sigma
        radius = jnp.maximum(jnp.floor((reach - 1.0) / tk) + 1.0, 0.0)
        radius = jnp.minimum(radius, float(nk)).astype(jnp.int32)[0, 0]
        jlo = jnp.maximum(jd - radius, 0)
        jhi = jnp.minimum(jd + radius, nk - 1)
        odd = lax.rem(jhi - jlo + 1, 2) == 1
        grow_hi = jnp.logical_and(odd, jhi < nk - 1)
        grow_lo = jnp.logical_and(odd, jhi >= nk - 1)
        jhi = jnp.where(grow_hi, jhi + 1, jhi)
        jlo = jnp.where(grow_lo, jlo - 1, jlo)
        nb = jhi - jlo + 1
    else:
        jlo, nb = 0, nk

    ql = lax.broadcasted_iota(jnp.int32, (1, tq2), 1)
    qpos = (q0 + jnp.where(ql < tq, ql, ql - tq)).astype(F32)
    ones_rows = jnp.where(
        lax.broadcasted_iota(jnp.int32, (V7X_BF16_ROWS, tk), 0) == 0, 1.0, 0.0).astype(BF16)

    def block_of(n):
        j = jlo + n - 1
        return jnp.where(n == 0, jd, jnp.where(j < jd, j, j + 1))

    lane_chunks = [pl.ds(c * V7X_MXU_WIDTH, V7X_MXU_WIDTH) for c in range(tq2 // V7X_MXU_WIDTH)]
    kk = lax.broadcasted_iota(jnp.int32, (tk, V7X_MXU_WIDTH), 0)
    qq = lax.broadcasted_iota(jnp.int32, (tk, V7X_MXU_WIDTH), 1)

    def keys_of(n):
        j = block_of(n)
        right = (j > jd).astype(jnp.int32)
        start = pl.multiple_of(j * tk, tk)
        ka = jnp.concatenate([k_ref[0, pl.ds(start, tk), :], kaug_ref[right]], axis=1)
        sgn = jnp.where(j > jd, jnp.float32(-1.0), jnp.float32(1.0))
        beta = (((j * tk).astype(F32) - qpos) * sigma) * sgn
        return ka, beta

    def scores_chunk(ka, beta, slot, c, diagonal=False):
        cs = lane_chunks[c]
        st = lax.dot_general(ka, qa_ref[cs, :], (((1,), (1,)), ((), ())),
                             preferred_element_type=F32)
        if diagonal:
            qoff = (c * V7X_MXU_WIDTH) % tq
            st = st + (jnp.maximum(kk - qq + (jd * tk - q0 - qoff), 0).astype(F32)
                       * sigma) * -2.0
        s_ref[slot, :, cs] = st
        mx_ref[slot, :, cs] = jnp.max(st, axis=0, keepdims=True) + beta[:, c * V7X_MXU_WIDTH:
                                                                      (c + 1) * V7X_MXU_WIDTH]

    def softmax_chunk(slot, c):
        cs = lane_chunks[c]
        m_prev = m_ref[:, cs]
        m_new = jnp.maximum(m_prev, mx_ref[slot, :, cs])
        alpha_ref[slot, :, cs] = jnp.exp2(m_prev - m_new)
        p_ref[slot, :, cs] = jnp.exp2(s_ref[slot, :, cs]
                                      - (m_new - beta_ref[slot, :, cs])).astype(BF16)
        m_ref[:, cs] = m_new

    def pv_chunk(vt, slot, c):
        cs = lane_chunks[c]
        acc_ref[:, cs] = alpha_ref[slot, :, cs] * acc_ref[:, cs] + jnp.dot(
            vt, p_ref[slot, :, cs], preferred_element_type=F32)

    def values_of(n):
        start = pl.multiple_of(block_of(n) * tk, tk)
        return jnp.concatenate([vt_ref[:, pl.ds(start, tk)], ones_rows], axis=0)

    def pipeline_step(n_scores, n_softmax, n_pv, diagonal=False):
        if n_scores is not None:
            ka, beta = keys_of(n_scores[0])
            beta_ref[n_scores[1]] = beta
        if n_pv is not None:
            vt = values_of(n_pv[0])
        for c in range(len(lane_chunks)):
            if n_softmax is not None:
                softmax_chunk(n_softmax, c)
            if n_scores is not None:
                scores_chunk(ka, beta, n_scores[1], c, diagonal)
            if n_pv is not None:
                pv_chunk(vt, n_pv[1], c)

    m_ref[...] = jnp.full(m_ref.shape, -jnp.inf, F32)
    acc_ref[...] = jnp.zeros(acc_ref.shape, F32)
    pipeline_step((0, 0), None, None, diagonal=True)
    pipeline_step((1, 1), 0, None)

    def pair(t, carry):
        n = 2 * t
        pipeline_step((n + 2, 0), 1, (n, 0))
        pipeline_step((n + 3, 1), 0, (n + 1, 1))
        return carry

    if windowed:
        n_pairs = lax.div(nb, 2) - 1
        lax.fori_loop(0, lax.div(n_pairs, 2), lambda t, c: pair(2 * t + 1, pair(2 * t, c)), 0)

        @pl.when(lax.rem(n_pairs, 2) == 1)
        def _():
            pair(n_pairs - 1, 0)
    else:
        n_pairs = nk // 2 - 1
        lax.fori_loop(0, n_pairs // 2, lambda t, c: pair(2 * t + 1, pair(2 * t, c)), 0)
        if n_pairs % 2:
            pair(n_pairs - 1, 0)
    pipeline_step(None, 1, (nb - 2, 0))
    pipeline_step(None, None, (nb - 1, 1))

    lam = (jnp.exp(jnp.sum(lq1_ref[...] * lk1_ref[...], axis=-1, keepdims=True))
           - jnp.exp(jnp.sum(lq2_ref[...] * lk2_ref[...], axis=-1, keepdims=True))
           + LAM_INIT)
    acc = acc_ref[...]
    o = acc[:HEAD_WIDTH, :] * (1.0 / acc[HEAD_WIDTH:HEAD_WIDTH + 1, :])
    at = o[:, :tq] - lam * o[:, tq:]
    ms = jnp.mean(at * at, axis=0, keepdims=True)
    at = at * lax.rsqrt(ms + SUBLN_EPS) * g_ref[...] * (1.0 - LAM_INIT)
    o_ref[0] = at.T.astype(o_ref.dtype)


def _attention(q, k, vt, lq1, lk1, lq2, lk2, g_subln_col, tq, tk):
    b, s, _ = q.shape
    nk = s // tk
    assert tk % tq == 0 and nk % 2 == 0 and nk >= 2
    vec = lambda n: pl.BlockSpec((1, n), lambda bi, hi, qi: (0, 0))
    return pl.pallas_call(
        functools.partial(_attn_body, tq=tq, tk=tk, nk=nk,
                          windowed=nk > ATTN_WINDOW_MIN_BLOCKS),
        grid=(b, N_HEADS, s // tq),
        in_specs=[pl.BlockSpec((1, tq, HEAD_WIDTH), lambda bi, hi, qi: (bi, qi, hi)),
                  pl.BlockSpec((1, s, HEAD_WIDTH), lambda bi, hi, qi: (bi, 0, hi)),
                  pl.BlockSpec((HEAD_WIDTH, s), lambda bi, hi, qi: (hi, bi)),
                  vec(HEAD_DIM), vec(HEAD_DIM), vec(HEAD_DIM), vec(HEAD_DIM),
                  pl.BlockSpec((HEAD_WIDTH, 1), lambda bi, hi, qi: (0, 0))],
        out_specs=pl.BlockSpec((1, tq, HEAD_WIDTH), lambda bi, hi, qi: (bi, qi, hi)),
        out_shape=jax.ShapeDtypeStruct((b, s, ATTN_WIDTH), BF16),
        scratch_shapes=[pltpu.VMEM((2 * tq, 2 * HEAD_WIDTH), BF16),
                        pltpu.VMEM((2, tk, HEAD_WIDTH), BF16),
                        pltpu.VMEM((2, tk, 2 * tq), F32),
                        pltpu.VMEM((2, tk, 2 * tq), BF16),
                        pltpu.VMEM((2, 1, 2 * tq), F32),
                        pltpu.VMEM((2, 1, 2 * tq), F32),
                        pltpu.VMEM((2, 1, 2 * tq), F32),
                        pltpu.VMEM((1, 2 * tq), F32),
                        pltpu.VMEM((HEAD_WIDTH + V7X_BF16_ROWS, 2 * tq), F32),
                        pltpu.VMEM((1, 1), F32)],
        compiler_params=_params(("parallel", "parallel", "arbitrary")),
        name="diff_attention",
    )(q, k, vt, lq1, lk1, lq2, lk2, g_subln_col)


def _cpow(lre, lim, n):
    shape = jnp.broadcast_shapes(lre.shape, n.shape)
    pre = jnp.ones(shape, F32)
    pim = jnp.zeros(shape, F32)
    bre, bim = lre, lim
    for bit in range(POW_BITS):
        on = ((n >> bit) & 1) == 1
        mre = jnp.where(on, bre, 1.0)
        mim = jnp.where(on, bim, 0.0)
        pre, pim = pre * mre - pim * mim, pre * mim + pim * mre
        bre, bim = bre * bre - bim * bim, 2.0 * bre * bim
    return pre, pim


def _discretise(a_re, a_im, log_step):
    step = jnp.exp(log_step)
    mag = jnp.exp(a_re * step)
    lre = mag * jnp.cos(a_im * step)
    lim = mag * jnp.sin(a_im * step)
    den = a_re * a_re + a_im * a_im
    nr = lre - 1.0
    fre = (nr * a_re + lim * a_im) / den
    fim = (lim * a_re - nr * a_im) / den
    return lre, lim, fre, fim


def _ssm_weights_body(arow_re, arow_im, acol_re, acol_im, lsrow, lscol, bt_re, bt_im,
                      ct_re, ct_im, wt_ref, ws_ref, wo_ref, at_ref):
    t, cw = SSM_CHUNK, CHUNK_W
    lane_pair = lax.broadcasted_iota(jnp.int32, (1, PAIR_STATE), 1) // STATE
    row_pair = lax.broadcasted_iota(jnp.int32, (PAIR_STATE, 1), 0) // STATE

    ch = lax.broadcasted_iota(jnp.int32, (SSM_GROUP, LAG_W), 0)
    ln = lax.broadcasted_iota(jnp.int32, (SSM_GROUP, LAG_W), 1)
    tile_lag = jnp.where((ln % SSM_GROUP) == ch, 1.0, 0.0).astype(F32)

    step_rows = lax.broadcasted_iota(jnp.int32, (cw, 1), 0) // SSM_GROUP
    slot = lax.broadcasted_iota(jnp.int32, (1, LAG_W), 1) // SSM_GROUP
    step_lanes = lax.broadcasted_iota(jnp.int32, (1, cw), 1) // SSM_GROUP

    zt = [jnp.zeros((SSM_GROUP, LAG_W), F32) for _ in range(2)]
    for d in range(2):
        lre, lim, fre, fim = _discretise(arow_re[d, 0], arow_im[d, 0], lsrow[d, 0])
        bre, bim = bt_re[d, 0], bt_im[d, 0]
        bbre = fre * bre - fim * bim
        bbim = fre * bim + fim * bre
        expo = (t - 1 - step_rows) if d == 0 else step_rows
        pre, pim = _cpow(lre, lim, expo)
        tbre = jnp.tile(bbre, (t, 1))
        tbim = jnp.tile(bbim, (t, 1))
        inj_re = pre * tbre - pim * tbim
        inj_im = pre * tbim + pim * tbre
        for g2 in range(2):
            keep = lane_pair == g2
            rows = pl.ds(g2 * cw, cw)
            ws_ref[0, rows, pl.ds((2 * d) * PAIR_STATE, PAIR_STATE)] = (
                jnp.where(keep, inj_re, 0.0).astype(ws_ref.dtype))
            ws_ref[0, rows, pl.ds((2 * d + 1) * PAIR_STATE, PAIR_STATE)] = (
                jnp.where(keep, inj_im, 0.0).astype(ws_ref.dtype))
        dre, dim_ = _cpow(lre, lim, jnp.full((1, 1), t, jnp.int32))
        at_ref[0, pl.ds(2 * d, 1), :] = dre
        at_ref[0, pl.ds(2 * d + 1, 1), :] = dim_

        cre, cim, _, _ = _discretise(acol_re[d, 0], acol_im[d, 0], lscol[d, 0])
        ctl_re = jnp.dot(ct_re[d, 0], tile_lag, precision=HI, preferred_element_type=F32)
        ctl_im = jnp.dot(ct_im[d, 0], tile_lag, precision=HI, preferred_element_type=F32)
        lag = (slot - (t - 1)) if d == 0 else ((t - 1) - slot)
        valid = (lag >= 0) & (slot < 2 * t - 1)
        qre, qim = _cpow(cre, cim, jnp.maximum(lag, 0))
        r_re = jnp.where(valid, qre * ctl_re - qim * ctl_im, 0.0)
        r_im = jnp.where(valid, qre * ctl_im + qim * ctl_re, 0.0)
        for g2 in range(2):
            keep = lane_pair == g2
            zt[g2] = (zt[g2]
                      + jnp.dot(jnp.where(keep, bbre, 0.0), r_re, precision=HI,
                                preferred_element_type=F32)
                      - jnp.dot(jnp.where(keep, bbim, 0.0), r_im, precision=HI,
                                preferred_element_type=F32))

        out_pow = (step_lanes + 1) if d == 0 else (t - step_lanes)
        ore, oim = _cpow(cre, cim, out_pow)
        c_re = ctl_re[:, :cw]
        c_im = ctl_im[:, :cw]
        rd_re = ore * c_re - oim * c_im
        rd_im = -(ore * c_im + oim * c_re)
        for g2 in range(2):
            keep = row_pair == g2
            cols = pl.ds(g2 * cw, cw)
            wo_ref[0, pl.ds((2 * d) * PAIR_STATE, PAIR_STATE), cols] = (
                jnp.where(keep, rd_re, 0.0).astype(wo_ref.dtype))
            wo_ref[0, pl.ds((2 * d + 1) * PAIR_STATE, PAIR_STATE), cols] = (
                jnp.where(keep, rd_im, 0.0).astype(wo_ref.dtype))

    for g2 in range(2):
        for s in range(t):
            off = (t - 1 - s) * SSM_GROUP
            shifted = zt[g2] if off == 0 else pltpu.roll(zt[g2], LAG_W - off, 1)
            wt_ref[0, g2, pl.ds(s * SSM_GROUP, SSM_GROUP), :] = (
                shifted[:, :cw].astype(wt_ref.dtype))


def _ssm_weights(a_re, a_im, log_step, b_re, b_im, c_re, c_im):
    np_, g2p = N_GROUP_PAIRS, PAIR_STATE
    arow = lambda a: a.reshape(2, np_, 1, g2p)
    acol = lambda a: a.reshape(2, np_, g2p, 1)
    ls = jnp.repeat(log_step, STATE, axis=-1)
    bt = lambda w: (w.reshape(2, np_, 2, STATE, SSM_GROUP)
                    .transpose(0, 1, 4, 2, 3).reshape(2, np_, SSM_GROUP, g2p))
    ct = lambda w: (w.reshape(2, np_, 2, SSM_GROUP, STATE)
                    .transpose(0, 1, 2, 4, 3).reshape(2, np_, g2p, SSM_GROUP))
    ins = [arow(a_re), arow(a_im), acol(a_re), acol(a_im), arow(ls), acol(ls),
           bt(b_re), bt(b_im), ct(c_re), ct(c_im)]
    spec = lambda x: pl.BlockSpec((2, 1) + x.shape[2:], lambda k: (0, k, 0, 0))
    return pl.pallas_call(
        _ssm_weights_body,
        grid=(np_,),
        in_specs=[spec(x) for x in ins],
        out_specs=[pl.BlockSpec((1, 2, CHUNK_W, CHUNK_W), lambda k: (k, 0, 0, 0)),
                   pl.BlockSpec((1, 2 * CHUNK_W, N_SLABS * g2p), lambda k: (k, 0, 0)),
                   pl.BlockSpec((1, N_SLABS * g2p, 2 * CHUNK_W), lambda k: (k, 0, 0)),
                   pl.BlockSpec((1, N_SLABS, g2p), lambda k: (k, 0, 0))],
        out_shape=[jax.ShapeDtypeStruct((np_, 2, CHUNK_W, CHUNK_W), BF16),
                   jax.ShapeDtypeStruct((np_, 2 * CHUNK_W, N_SLABS * g2p), BF16),
                   jax.ShapeDtypeStruct((np_, N_SLABS * g2p, 2 * CHUNK_W), BF16),
                   jax.ShapeDtypeStruct((np_, N_SLABS, g2p), F32)],
        compiler_params=_params(("parallel",)),
        name="ssm_weights",
    )(*ins)


def _ssm_inject_body(x_ref, ws_ref, s_ref):
    s = jnp.dot(x_ref[0], ws_ref[0], preferred_element_type=F32)
    for slab in range(N_SLABS):
        s_ref[slab] = s[:, slab * PAIR_STATE:(slab + 1) * PAIR_STATE]


def _ssm_inject(x, ws):
    np_, r, _ = x.shape
    return pl.pallas_call(
        _ssm_inject_body,
        grid=(np_,),
        in_specs=[pl.BlockSpec((1, r, 2 * CHUNK_W), lambda k: (k, 0, 0)),
                  pl.BlockSpec((1,) + ws.shape[1:], lambda k: (k, 0, 0))],
        out_specs=pl.BlockSpec((N_SLABS, r, PAIR_STATE), lambda k: (0, 0, k)),
        out_shape=jax.ShapeDtypeStruct((N_SLABS, r, np_ * PAIR_STATE), F32),
        compiler_params=_params(("parallel",)),
        name="ssm_inject",
    )(x, ws)


def _ssm_scan_body(s_ref, at_ref, hin_ref, *, tiles_per_seq, n_tiles):
    sub = V7X_SUBLANES
    lanes = s_ref.shape[-1]
    a = [at_ref[pl.ds(slab, 1), :] for slab in range(N_SLABS)]

    def sweep(d, tile, carry):
        edge = 0 if d == 0 else tiles_per_seq - 1
        keep = jnp.where(lax.rem(tile, tiles_per_seq) == edge,
                         jnp.float32(0.0), jnp.float32(1.0))
        hre, him = carry[0] * keep, carry[1] * keep
        are, aim = a[2 * d], a[2 * d + 1]
        row0 = pl.multiple_of(tile * V7X_SUBLANES, V7X_SUBLANES)
        sre = s_ref[2 * d, pl.ds(row0, V7X_SUBLANES), :]
        sim = s_ref[2 * d + 1, pl.ds(row0, V7X_SUBLANES), :]
        out_re = [None] * sub
        out_im = [None] * sub
        order = range(sub) if d == 0 else range(sub - 1, -1, -1)
        for q in order:
            out_re[q], out_im[q] = hre, him
            xr = sre[q:q + 1]
            xi = sim[q:q + 1]
            hre, him = are * hre - aim * him + xr, are * him + aim * hre + xi
        hin_ref[2 * d, pl.ds(row0, V7X_SUBLANES), :] = jnp.concatenate(out_re, axis=0)
        hin_ref[2 * d + 1, pl.ds(row0, V7X_SUBLANES), :] = jnp.concatenate(out_im, axis=0)
        return hre, him

    zero = (jnp.zeros((1, lanes), F32), jnp.zeros((1, lanes), F32))
    lax.fori_loop(0, n_tiles, lambda i, c: sweep(0, i, c), zero)
    lax.fori_loop(0, n_tiles, lambda i, c: sweep(1, n_tiles - 1 - i, c), zero)


def _ssm_scan(s, at, chunks_per_seq, lane_block):
    _, r, lanes = s.shape
    assert chunks_per_seq % V7X_SUBLANES == 0 and r % chunks_per_seq == 0
    return pl.pallas_call(
        functools.partial(_ssm_scan_body, tiles_per_seq=chunks_per_seq // V7X_SUBLANES,
                          n_tiles=r // V7X_SUBLANES),
        grid=(lanes // lane_block,),
        in_specs=[pl.BlockSpec((N_SLABS, r, lane_block), lambda k: (0, 0, k)),
                  pl.BlockSpec((N_SLABS, lane_block), lambda k: (0, k))],
        out_specs=pl.BlockSpec((N_SLABS, r, lane_block), lambda k: (0, 0, k)),
        out_shape=jax.ShapeDtypeStruct(s.shape, F32),
        compiler_params=_params(("parallel",)),
        name="ssm_scan",
    )(s, at)


def _ssm_mix_body(x_ref, wt_ref, hin_ref, wo_ref, y_ref):
    x = x_ref[0]
    hin = jnp.concatenate([hin_ref[slab] for slab in range(N_SLABS)], axis=1).astype(BF16)
    carried = jnp.dot(hin, wo_ref[0], preferred_element_type=F32)
    for g2 in range(2):
        cols = slice(g2 * CHUNK_W, (g2 + 1) * CHUNK_W)
        y_ref[0, :, cols] = (carried[:, cols] + jnp.dot(
            x[:, cols], wt_ref[0, g2], preferred_element_type=F32)).astype(y_ref.dtype)


def _ssm_mix(x, wt, hin, wo):
    np_, r, _ = x.shape
    return pl.pallas_call(
        _ssm_mix_body,
        grid=(np_,),
        in_specs=[pl.BlockSpec((1, r, 2 * CHUNK_W), lambda k: (k, 0, 0)),
                  pl.BlockSpec((1,) + wt.shape[1:], lambda k: (k, 0, 0, 0)),
                  pl.BlockSpec((N_SLABS, r, PAIR_STATE), lambda k: (0, 0, k)),
                  pl.BlockSpec((1,) + wo.shape[1:], lambda k: (k, 0, 0))],
        out_specs=pl.BlockSpec((1, r, 2 * CHUNK_W), lambda k: (k, 0, 0)),
        out_shape=jax.ShapeDtypeStruct((np_, r, 2 * CHUNK_W), BF16),
        compiler_params=_params(("parallel",)),
        name="ssm_mix",
    )(x, wt, hin, wo)


def _outproj_body(x_ref, a_ref, yc_ref, u_ref, d_ref, wg_ref, bg_ref, gs_ref, wo_ref, gf_ref,
                  laneperm_t_ref, rowperm_t_ref, x1_ref, h_ref):
    y_ssm = _from_chunk_layout(yc_ref, laneperm_t_ref, rowperm_t_ref)
    y = jax.nn.gelu(y_ssm + d_ref[...] * u_ref[...])
    gate = jnp.dot(y.astype(BF16), wg_ref[...], preferred_element_type=F32) + bg_ref[...]
    s = _rms(y * jax.nn.sigmoid(gate), gs_ref[...], NORM_EPS)
    mixed = (jnp.dot(a_ref[...], wo_ref[:ATTN_WIDTH, :], preferred_element_type=F32)
             + jnp.dot(s.astype(BF16), wo_ref[ATTN_WIDTH:, :], preferred_element_type=F32))
    x1 = x_ref[...] + mixed
    x1_ref[...] = x1
    h_ref[...] = _rms(x1, gf_ref[...], NORM_EPS).astype(h_ref.dtype)


def _outproj(x2, a2, yc, u2, d, w_glu, b_glu, g_ssm, w_out, g_ffn, laneperm_t, rowperm_t):
    n_tok = x2.shape[0]
    tm = RELAYOUT_ROWS
    row = lambda w: pl.BlockSpec((tm, w), lambda i: (i, 0))
    full = lambda x: pl.BlockSpec(x.shape, lambda i: (0, 0))
    chunk_block = pl.BlockSpec((N_GROUP_PAIRS, CHUNKS_PER_TILE, 2 * CHUNK_W), lambda i: (0, i, 0))
    return pl.pallas_call(
        _outproj_body,
        grid=(n_tok // tm,),
        in_specs=[row(D_MODEL), row(ATTN_WIDTH), chunk_block, row(SSM_WIDTH),
                  full(d), full(w_glu), full(b_glu), full(g_ssm), full(w_out), full(g_ffn),
                  full(laneperm_t), full(rowperm_t)],
        out_specs=[row(D_MODEL), row(D_MODEL)],
        out_shape=[jax.ShapeDtypeStruct((n_tok, D_MODEL), F32),
                   jax.ShapeDtypeStruct((n_tok, D_MODEL), BF16)],
        compiler_params=_params(("parallel",)),
        name="outproj",
    )(x2, a2, yc, u2, d, w_glu, b_glu, g_ssm, w_out, g_ffn, laneperm_t, rowperm_t)


def _ffn_body(h_ref, hp_ref, hn_ref, x1_ref, wu_ref, cw_ref, cb_ref, wd_ref, gf_ref, o_ref,
              hcat_ref, *, tm, tiles_per_seq, ff_chunk):
    i = pl.program_id(0)
    halo = V7X_BF16_ROWS
    first = (i % tiles_per_seq) == 0
    last = (i % tiles_per_seq) == tiles_per_seq - 1
    zeros = jnp.zeros((halo, D_MODEL), hcat_ref.dtype)
    hcat_ref[:halo, :] = jnp.where(first, zeros, hp_ref[...])
    hcat_ref[halo:halo + tm, :] = h_ref[...]
    hcat_ref[halo + tm:, :] = jnp.where(last, zeros, hn_ref[...])
    hcat = hcat_ref[...]
    rows = tm + 2 * halo

    def conv(z, col0):
        cols = pl.ds(col0, ff_chunk)
        prev = pltpu.roll(z, 1, 0)[halo:halo + tm]
        nxt = pltpu.roll(z, rows - 1, 0)[halo:halo + tm]
        return (prev * cw_ref[pl.ds(0, 1), cols] + z[halo:halo + tm] * cw_ref[pl.ds(1, 1), cols]
                + nxt * cw_ref[pl.ds(2, 1), cols] + cb_ref[:, cols])

    def up(c):
        g0 = c * ff_chunk
        return (jnp.dot(hcat, wu_ref[:, pl.ds(g0, ff_chunk)], preferred_element_type=F32),
                jnp.dot(hcat, wu_ref[:, pl.ds(D_FF + g0, ff_chunk)],
                        preferred_element_type=F32))

    n_chunks = D_FF // ff_chunk
    acc = jnp.zeros((tm, D_MODEL), F32)
    z_next = up(0)
    for c in range(n_chunks):
        zg, zv = z_next
        if c + 1 < n_chunks:
            z_next = up(c + 1)
        g0 = c * ff_chunk
        act = (jax.nn.gelu(conv(zg, g0)) * conv(zv, D_FF + g0)).astype(BF16)
        acc = acc + jnp.dot(act, wd_ref[pl.ds(g0, ff_chunk), :], preferred_element_type=F32)
    o_ref[...] = _rms(x1_ref[...] + acc, gf_ref[...], NORM_EPS)


def _ffn(h2, x1, w_up, conv_w, conv_b, w_down, g_final, tm, seq, ff_chunk):
    n_tok = h2.shape[0]
    halo = V7X_BF16_ROWS
    per = tm // halo
    n_halo = n_tok // halo
    row = lambda w: pl.BlockSpec((tm, w), lambda i: (i, 0))
    once = lambda x: pl.BlockSpec(x.shape, lambda i: (0, 0), pipeline_mode=pl.Buffered(1))
    return pl.pallas_call(
        functools.partial(_ffn_body, tm=tm, tiles_per_seq=seq // tm, ff_chunk=ff_chunk),
        grid=(n_tok // tm,),
        in_specs=[row(D_MODEL),
                  pl.BlockSpec((halo, D_MODEL), lambda i: (jnp.maximum(i * per - 1, 0), 0)),
                  pl.BlockSpec((halo, D_MODEL),
                               lambda i: (jnp.minimum((i + 1) * per, n_halo - 1), 0)),
                  row(D_MODEL), once(w_up), once(conv_w), once(conv_b), once(w_down),
                  once(g_final)],
        out_specs=row(D_MODEL),
        out_shape=jax.ShapeDtypeStruct((n_tok, D_MODEL), F32),
        scratch_shapes=[pltpu.VMEM((tm + 2 * halo, D_MODEL), BF16)],
        compiler_params=_params(("parallel",)),
        name="ffn",
    )(h2, h2, h2, x1, w_up, conv_w, conv_b, w_down, g_final)


def _tile(pref, n):
    t = min(pref, n)
    assert n % t == 0, (pref, n)
    return t


def _trunk(x, p, ssm_w, perms):
    batch, seq, _ = x.shape
    n_tok = batch * seq
    x2 = x.reshape(n_tok, D_MODEL)
    assert seq % RELAYOUT_ROWS == 0
    rowperm, laneperm, laneperm_t, rowperm_t = perms

    q, k, vt, u, xc = _inproj(x2, p["g_mix"], p["w_qku"], p["w_vt"], rowperm, laneperm)

    shape3 = (batch, seq, ATTN_WIDTH)
    a = _attention(q.reshape(shape3), k.reshape(shape3), vt,
                   p["lq1"], p["lk1"], p["lq2"], p["lk2"], p["g_subln"],
                   _tile(ATTN_Q_TILE, seq), _tile(ATTN_K_TILE, seq))

    wt, ws, wo, at = ssm_w
    s_in = _ssm_inject(xc, ws)
    at_flat = at.transpose(1, 0, 2).reshape(N_SLABS, N_GROUP_PAIRS * PAIR_STATE)
    hin = _ssm_scan(s_in, at_flat, seq // SSM_CHUNK, 4 * PAIR_STATE)
    yc = _ssm_mix(xc, wt, hin, wo)

    x1, h = _outproj(x2, a.reshape(n_tok, ATTN_WIDTH), yc, u, p["d"], p["w_glu"], p["b_glu"],
                     p["g_ssm"], p["w_out"], p["g_ffn"], laneperm_t, rowperm_t)
    out = _ffn(h, x1, p["w_up"], p["conv_w"], p["conv_b"], p["w_down"], p["g_final"],
               _tile(FFN_ROW_TILE, seq), seq, FFN_COL_CHUNK)
    return out.reshape(batch, seq, D_MODEL)


def kernel(x_prompt, x_sample, g_mix_norm, w_in, lambda_q1, lambda_k1, lambda_q2, lambda_k2, g_subln, ssm_a_re, ssm_a_im, ssm_log_step, ssm_b_re, ssm_b_im, ssm_c_re, ssm_c_im, ssm_d, w_glu, b_glu, g_ssm_out, w_out, g_ffn_norm, w_up, conv_w, conv_b, w_down, g_final):
    layer = 0
    vec = lambda t: t.reshape(1, -1).astype(F32)
    p = {
        "g_mix": vec(g_mix_norm[layer]),
        "w_qku": jnp.concatenate([w_in[layer][:, :2 * ATTN_WIDTH],
                                  w_in[layer][:, 3 * ATTN_WIDTH:]], axis=1).astype(BF16),
        "w_vt": w_in[layer][:, 2 * ATTN_WIDTH:3 * ATTN_WIDTH].T.astype(BF16),
        "lq1": vec(lambda_q1[layer]), "lk1": vec(lambda_k1[layer]),
        "lq2": vec(lambda_q2[layer]), "lk2": vec(lambda_k2[layer]),
        "g_subln": g_subln[layer].reshape(-1, 1).astype(F32), "d": vec(ssm_d[layer]),
        "w_glu": w_glu[layer].astype(BF16), "b_glu": vec(b_glu[layer]),
        "g_ssm": vec(g_ssm_out[layer]), "w_out": w_out[layer].astype(BF16),
        "g_ffn": vec(g_ffn_norm[layer]), "w_up": w_up[layer].astype(BF16),
        "conv_w": conv_w[layer].astype(F32), "conv_b": vec(conv_b[layer]),
        "w_down": w_down[layer].astype(BF16), "g_final": vec(g_final),
    }
    f32 = lambda t: t[layer].astype(F32)
    ssm_w = _ssm_weights(f32(ssm_a_re), f32(ssm_a_im), f32(ssm_log_step), f32(ssm_b_re),
                         f32(ssm_b_im), f32(ssm_c_re), f32(ssm_c_im))
    perms = _relayout_constants()
    return _trunk(x_prompt, p, ssm_w, perms), _trunk(x_sample, p, ssm_w, perms)
```

```python
import functools
import math

import jax
import jax.numpy as jnp
import numpy as np
from jax import lax
from jax.experimental import pallas as pl
from jax.experimental.pallas import tpu as pltpu

F32 = jnp.float32
BF16 = jnp.bfloat16

D_MODEL = 1024
ATTN_WIDTH = 512
SSM_WIDTH = 512
N_HEADS = 4
HEAD_DIM = 64
HEAD_WIDTH = 2 * HEAD_DIM
SSM_GROUP = 16
N_SSM_GROUPS = 32
N_GROUP_PAIRS = N_SSM_GROUPS // 2
STATE = 64
PAIR_STATE = 2 * STATE
D_FF = 2816
NORM_EPS = 1e-6
SUBLN_EPS = 1e-5
LAM_INIT = 0.8 - 0.6 * math.exp(-0.3 * 0)
LOG2E = math.log2(math.e)
N_BIAS_LANES = 3
ATTN_UNDERFLOW_LOG2 = 152.0
ATTN_NORM_SLACK = 1.001
ATTN_WINDOW_MIN_BLOCKS = 8
ATTN_Q_TILE = 512
ATTN_K_TILE = 512
FFN_ROW_TILE = 512
FFN_COL_CHUNK = D_FF // 2

SSM_CHUNK = 32
CHUNK_W = SSM_CHUNK * SSM_GROUP
LAG_W = 2 * CHUNK_W
POW_BITS = SSM_CHUNK.bit_length()
N_SLABS = 4

V7X_MXU_WIDTH = 256
V7X_SUBLANES = 8
V7X_BF16_ROWS = 16
V7X_VMEM_LIMIT = 56 * 1024 * 1024

HI = lax.Precision.HIGHEST


def _params(sem):
    return pltpu.CompilerParams(dimension_semantics=sem, vmem_limit_bytes=V7X_VMEM_LIMIT)


def _rms(x, g, eps):
    return x * lax.rsqrt(jnp.mean(x * x, axis=-1, keepdims=True) + eps) * g


RELAYOUT_ROWS = 512
CHUNKS_PER_TILE = RELAYOUT_ROWS // SSM_CHUNK
LANE = 128
STEP_LO = LANE // SSM_GROUP
STEP_HI = SSM_CHUNK // STEP_LO
LANE_TILES = SSM_WIDTH // LANE
PERM_W = STEP_LO * LANE


def _relayout_constants():
    rows = np.zeros((RELAYOUT_ROWS, RELAYOUT_ROWS), np.float32)
    for c in range(CHUNKS_PER_TILE):
        for s in range(SSM_CHUNK):
            rows[s * CHUNKS_PER_TILE + c, c * SSM_CHUNK + s] = 1.0
    lanes = np.zeros((PERM_W, PERM_W), np.float32)
    for s_lo in range(STEP_LO):
        for g8 in range(STEP_LO):
            for i in range(SSM_GROUP):
                lanes[s_lo * LANE + g8 * SSM_GROUP + i, g8 * LANE + s_lo * SSM_GROUP + i] = 1.0
    as_bf = lambda m: jnp.asarray(m, BF16)
    return as_bf(rows), as_bf(lanes), as_bf(lanes.T), as_bf(rows.T)


def _pair_lanes(lane_tile, g8, step_hi):
    group = lane_tile * STEP_LO + g8
    return group // 2, (group % 2) * CHUNK_W + step_hi * LANE


def _to_chunk_layout(u, rowperm_ref, laneperm_ref, x_ref):
    nct = CHUNKS_PER_TILE
    up = jnp.dot(rowperm_ref[...], u.astype(BF16), preferred_element_type=F32).astype(BF16)
    stacked = jnp.concatenate(
        [jnp.concatenate([up[(hi * STEP_LO + lo) * nct:(hi * STEP_LO + lo + 1) * nct,
                             lt * LANE:(lt + 1) * LANE] for lo in range(STEP_LO)], axis=1)
         for lt in range(LANE_TILES) for hi in range(STEP_HI)], axis=0)
    out = jnp.dot(stacked, laneperm_ref[...], preferred_element_type=F32).astype(BF16)
    for lt in range(LANE_TILES):
        for hi in range(STEP_HI):
            r0 = (lt * STEP_HI + hi) * nct
            for g8 in range(STEP_LO):
                pair, lane0 = _pair_lanes(lt, g8, hi)
                x_ref[pair, :, pl.ds(lane0, LANE)] = out[r0:r0 + nct, g8 * LANE:(g8 + 1) * LANE]


def _from_chunk_layout(y_ref, laneperm_t_ref, rowperm_t_ref):
    nct = CHUNKS_PER_TILE
    pieces = []
    for lt in range(LANE_TILES):
        for hi in range(STEP_HI):
            row = []
            for g8 in range(STEP_LO):
                pair, lane0 = _pair_lanes(lt, g8, hi)
                row.append(y_ref[pair, :, pl.ds(lane0, LANE)])
            pieces.append(jnp.concatenate(row, axis=1))
    stacked = jnp.concatenate(pieces, axis=0)
    z = jnp.dot(stacked.astype(BF16), laneperm_t_ref[...],
                preferred_element_type=F32).astype(BF16)
    perm = jnp.concatenate(
        [jnp.concatenate([z[(lt * STEP_HI + hi) * nct:(lt * STEP_HI + hi + 1) * nct,
                            lo * LANE:(lo + 1) * LANE] for lt in range(LANE_TILES)], axis=1)
         for hi in range(STEP_HI) for lo in range(STEP_LO)], axis=0)
    return jnp.dot(rowperm_t_ref[...], perm, preferred_element_type=F32)


def _inproj_body(x_ref, g_ref, w_ref, rowperm_ref, laneperm_ref,
                 q_ref, k_ref, vt_ref, u_ref, xc_ref):
    n = _rms(x_ref[...], g_ref[...], NORM_EPS).astype(BF16)
    proj = jnp.dot(n, w_ref[...], preferred_element_type=F32)
    a = ATTN_WIDTH
    q_ref[...] = (proj[:, :a] * (LOG2E * HEAD_DIM ** -0.5)).astype(BF16)
    k_ref[...] = proj[:, a:2 * a].astype(BF16)
    vt_ref[...] = proj[:, 2 * a:3 * a].T.astype(BF16)
    u = proj[:, 3 * a:]
    u_ref[...] = u
    _to_chunk_layout(u, rowperm_ref, laneperm_ref, xc_ref)


def _inproj(x2, g, w_in, rowperm, laneperm):
    n_tok = x2.shape[0]
    tm = RELAYOUT_ROWS
    row = lambda w: pl.BlockSpec((tm, w), lambda i: (i, 0))
    full = lambda x: pl.BlockSpec(x.shape, lambda i: (0, 0))
    chunk_block = pl.BlockSpec((N_GROUP_PAIRS, CHUNKS_PER_TILE, 2 * CHUNK_W), lambda i: (0, i, 0))
    return pl.pallas_call(
        _inproj_body,
        grid=(n_tok // tm,),
        in_specs=[row(D_MODEL), full(g), full(w_in), full(rowperm), full(laneperm)],
        out_specs=[row(ATTN_WIDTH), row(ATTN_WIDTH),
                   pl.BlockSpec((ATTN_WIDTH, tm), lambda i: (0, i)), row(SSM_WIDTH), chunk_block],
        out_shape=[jax.ShapeDtypeStruct((n_tok, ATTN_WIDTH), BF16),
                   jax.ShapeDtypeStruct((n_tok, ATTN_WIDTH), BF16),
                   jax.ShapeDtypeStruct((ATTN_WIDTH, n_tok), BF16),
                   jax.ShapeDtypeStruct((n_tok, SSM_WIDTH), F32),
                   jax.ShapeDtypeStruct((N_GROUP_PAIRS, n_tok // SSM_CHUNK, 2 * CHUNK_W), BF16)],
        compiler_params=_params(("parallel",)),
        name="inproj",
    )(x2, g, w_in, rowperm, laneperm)


def _max_half_norm_sq(x):
    sq = x.astype(F32)
    sq = sq * sq
    lane = lax.broadcasted_iota(jnp.int32, sq.shape, 1)
    first = jnp.sum(jnp.where(lane < HEAD_DIM, sq, 0.0), axis=1, keepdims=True)
    second = jnp.sum(jnp.where(lane < HEAD_DIM, 0.0, sq), axis=1, keepdims=True)
    return jnp.max(jnp.maximum(first, second), axis=0, keepdims=True)


def _attn_body(q_ref, k_ref, vt_ref, lq1_ref, lk1_ref, lq2_ref, lk2_ref, g_ref, o_ref,
               qa_ref, kaug_ref, s_ref, p_ref, mx_ref, beta_ref, alpha_ref, m_ref, acc_ref,
               knorm_ref, *, tq, tk, nk, windowed):
    h = pl.program_id(1)
    i = pl.program_id(2)
    tq2 = 2 * tq
    sigma = jnp.float32(0.0)
    for head in range(N_HEADS):
        sigma = jnp.where(h == head,
                          jnp.float32(LOG2E * 2.0 ** (-8.0 * (head + 1) / N_HEADS)), sigma)
    q0 = i * tq
    jd = q0 // tk

    q = q_ref[0]
    lane = lax.broadcasted_iota(jnp.int32, q.shape, 1)
    zero = jnp.zeros_like(q)
    ones3 = jnp.where(lane < N_BIAS_LANES, 1.0, 0.0).astype(BF16)
    qa_ref[:tq, :HEAD_WIDTH] = jnp.where(lane < HEAD_DIM, q, zero)
    qa_ref[tq:, :HEAD_WIDTH] = jnp.where(lane < HEAD_DIM, zero, q)
    qa_ref[:tq, HEAD_WIDTH:] = ones3
    qa_ref[tq:, HEAD_WIDTH:] = ones3

    @pl.when(i == 0)
    def _():
        koff = lax.broadcasted_iota(jnp.int32, (tk, HEAD_WIDTH), 0).astype(F32) * sigma
        klane = lax.broadcasted_iota(jnp.int32, (tk, HEAD_WIDTH), 1)
        hi = koff.astype(BF16).astype(F32)
        mid = (koff - hi).astype(BF16).astype(F32)
        lo = koff - hi - mid
        ktile = jnp.where(klane == 0, hi,
                          jnp.where(klane == 1, mid, jnp.where(klane == 2, lo, 0.0)))
        kaug_ref[0] = ktile.astype(BF16)
        kaug_ref[1] = (-ktile).astype(BF16)

        if windowed:
            def knorm(c, best):
                rows = k_ref[0, pl.ds(pl.multiple_of(c * tk, tk), tk), :]
                return jnp.maximum(best, _max_half_norm_sq(rows))

            knorm_ref[...] = lax.fori_loop(0, nk, knorm, jnp.zeros((1, 1), F32))

    if windowed:
        qk = jnp.sqrt(_max_half_norm_sq(q) * knorm_ref[...]) * ATTN_NORM_SLACK
        reach = (2.0 * qk + ATTN_UNDERFLOW_LOG2) / sigma
        radius = jnp.maximum(jnp.floor((reach - 1.0) / tk) + 1.0, 0.0)
        radius = jnp.minimum(radius, float(nk)).astype(jnp.int32)[0, 0]
        jlo = jnp.maximum(jd - radius, 0)
        jhi = jnp.minimum(jd + radius, nk - 1)
        odd = lax.rem(jhi - jlo + 1, 2) == 1
        grow_hi = jnp.logical_and(odd, jhi < nk - 1)
        grow_lo = jnp.logical_and(odd, jhi >= nk - 1)
        jhi = jnp.where(grow_hi, jhi + 1, jhi)
        jlo = jnp.where(grow_lo, jlo - 1, jlo)
        nb = jhi - jlo + 1
    else:
        jlo, nb = 0, nk

    ql = lax.broadcasted_iota(jnp.int32, (1, tq2), 1)
    qpos = (q0 + jnp.where(ql < tq, ql, ql - tq)).astype(F32)
    ones_rows = jnp.where(
        lax.broadcasted_iota(jnp.int32, (V7X_BF16_ROWS, tk), 0) == 0, 1.0, 0.0).astype(BF16)

    def block_of(n):
        j = jlo + n - 1
        return jnp.where(n == 0, jd, jnp.where(j < jd, j, j + 1))

    lane_chunks = [pl.ds(c * V7X_MXU_WIDTH, V7X_MXU_WIDTH) for c in range(tq2 // V7X_MXU_WIDTH)]
    kk = lax.broadcasted_iota(jnp.int32, (tk, V7X_MXU_WIDTH), 0)
    qq = lax.broadcasted_iota(jnp.int32, (tk, V7X_MXU_WIDTH), 1)

    def keys_of(n):
        j = block_of(n)
        right = (j > jd).astype(jnp.int32)
        start = pl.multiple_of(j * tk, tk)
        ka = jnp.concatenate([k_ref[0, pl.ds(start, tk), :], kaug_ref[right]], axis=1)
        sgn = jnp.where(j > jd, jnp.float32(-1.0), jnp.float32(1.0))
        beta = (((j * tk).astype(F32) - qpos) * sigma) * sgn
        return ka, beta

    def scores_chunk(ka, beta, slot, c, diagonal=False):
        cs = lane_chunks[c]
        st = lax.dot_general(ka, qa_ref[cs, :], (((1,), (1,)), ((), ())),
                             preferred_element_type=F32)
        if diagonal:
            qoff = (c * V7X_MXU_WIDTH) % tq
            st = st + (jnp.maximum(kk - qq + (jd * tk - q0 - qoff), 0).astype(F32)
                       * sigma) * -2.0
        s_ref[slot, :, cs] = st
        mx_ref[slot, :, cs] = jnp.max(st, axis=0, keepdims=True) + beta[:, c * V7X_MXU_WIDTH:
                                                                      (c + 1) * V7X_MXU_WIDTH]

    def softmax_chunk(slot, c):
        cs = lane_chunks[c]
        m_prev = m_ref[:, cs]
        m_new = jnp.maximum(m_prev, mx_ref[slot, :, cs])
        alpha_ref[slot, :, cs] = jnp.exp2(m_prev - m_new)
        p_ref[slot, :, cs] = jnp.exp2(s_ref[slot, :, cs]
                                      - (m_new - beta_ref[slot, :, cs])).astype(BF16)
        m_ref[:, cs] = m_new

    def pv_chunk(vt, slot, c):
        cs = lane_chunks[c]
        acc_ref[:, cs] = alpha_ref[slot, :, cs] * acc_ref[:, cs] + jnp.dot(
            vt, p_ref[slot, :, cs], preferred_element_type=F32)

    def values_of(n):
        start = pl.multiple_of(block_of(n) * tk, tk)
        return jnp.concatenate([vt_ref[:, pl.ds(start, tk)], ones_rows], axis=0)

    def pipeline_step(n_scores, n_softmax, n_pv, diagonal=False):
        if n_scores is not None:
            ka, beta = keys_of(n_scores[0])
            beta_ref[n_scores[1]] = beta
        if n_pv is not None:
            vt = values_of(n_pv[0])
        for c in range(len(lane_chunks)):
            if n_softmax is not None:
                softmax_chunk(n_softmax, c)
            if n_scores is not None:
                scores_chunk(ka, beta, n_scores[1], c, diagonal)
            if n_pv is not None:
                pv_chunk(vt, n_pv[1], c)

    m_ref[...] = jnp.full(m_ref.shape, -jnp.inf, F32)
    acc_ref[...] = jnp.zeros(acc_ref.shape, F32)
    pipeline_step((0, 0), None, None, diagonal=True)
    pipeline_step((1, 1), 0, None)

    def pair(t, carry):
        n = 2 * t
        pipeline_step((n + 2, 0), 1, (n, 0))
        pipeline_step((n + 3, 1), 0, (n + 1, 1))
        return carry

    if windowed:
        n_pairs = lax.div(nb, 2) - 1
        lax.fori_loop(0, lax.div(n_pairs, 2), lambda t, c: pair(2 * t + 1, pair(2 * t, c)), 0)

        @pl.when(lax.rem(n_pairs, 2) == 1)
        def _():
            pair(n_pairs - 1, 0)
    else:
        n_pairs = nk // 2 - 1
        lax.fori_loop(0, n_pairs // 2, lambda t, c: pair(2 * t + 1, pair(2 * t, c)), 0)
        if n_pairs % 2:
            pair(n_pairs - 1, 0)
    pipeline_step(None, 1, (nb - 2, 0))
    pipeline_step(None, None, (nb - 1, 1))

    lam = (jnp.exp(jnp.sum(lq1_ref[...] * lk1_ref[...], axis=-1, keepdims=True))
           - jnp.exp(jnp.sum(lq2_ref[...] * lk2_ref[...], axis=-1, keepdims=True))
           + LAM_INIT)
    acc = acc_ref[...]
    o = acc[:HEAD_WIDTH, :] * (1.0 / acc[HEAD_WIDTH:HEAD_WIDTH + 1, :])
    at = o[:, :tq] - lam * o[:, tq:]
    ms = jnp.mean(at * at, axis=0, keepdims=True)
    at = at * lax.rsqrt(ms + SUBLN_EPS) * g_ref[...] * (1.0 - LAM_INIT)
    o_ref[0] = at.T.astype(o_ref.dtype)


def _attention(q, k, vt, lq1, lk1, lq2, lk2, g_subln_col, tq, tk):
    b, s, _ = q.shape
    nk = s // tk
    assert tk % tq == 0 and nk % 2 == 0 and nk >= 2
    vec = lambda n: pl.BlockSpec((1, n), lambda bi, hi, qi: (0, 0))
    return pl.pallas_call(
        functools.partial(_attn_body, tq=tq, tk=tk, nk=nk,
                          windowed=nk > ATTN_WINDOW_MIN_BLOCKS),
        grid=(b, N_HEADS, s // tq),
        in_specs=[pl.BlockSpec((1, tq, HEAD_WIDTH), lambda bi, hi, qi: (bi, qi, hi)),
                  pl.BlockSpec((1, s, HEAD_WIDTH), lambda bi, hi, qi: (bi, 0, hi)),
                  pl.BlockSpec((HEAD_WIDTH, s), lambda bi, hi, qi: (hi, bi)),
                  vec(HEAD_DIM), vec(HEAD_DIM), vec(HEAD_DIM), vec(HEAD_DIM),
                  pl.BlockSpec((HEAD_WIDTH, 1), lambda bi, hi, qi: (0, 0))],
        out_specs=pl.BlockSpec((1, tq, HEAD_WIDTH), lambda bi, hi, qi: (bi, qi, hi)),
        out_shape=jax.ShapeDtypeStruct((b, s, ATTN_WIDTH), BF16),
        scratch_shapes=[pltpu.VMEM((2 * tq, 2 * HEAD_WIDTH), BF16),
                        pltpu.VMEM((2, tk, HEAD_WIDTH), BF16),
                        pltpu.VMEM((2, tk, 2 * tq), F32),
                        pltpu.VMEM((2, tk, 2 * tq), BF16),
                        pltpu.VMEM((2, 1, 2 * tq), F32),
                        pltpu.VMEM((2, 1, 2 * tq), F32),
                        pltpu.VMEM((2, 1, 2 * tq), F32),
                        pltpu.VMEM((1, 2 * tq), F32),
                        pltpu.VMEM((HEAD_WIDTH + V7X_BF16_ROWS, 2 * tq), F32),
                        pltpu.VMEM((1, 1), F32)],
        compiler_params=_params(("parallel", "parallel", "arbitrary")),
        name="diff_attention",
    )(q, k, vt, lq1, lk1, lq2, lk2, g_subln_col)


def _cpow(lre, lim, n):
    shape = jnp.broadcast_shapes(lre.shape, n.shape)
    pre = jnp.ones(shape, F32)
    pim = jnp.zeros(shape, F32)
    bre, bim = lre, lim
    for bit in range(POW_BITS):
        on = ((n >> bit) & 1) == 1
        mre = jnp.where(on, bre, 1.0)
        mim = jnp.where(on, bim, 0.0)
        pre, pim = pre * mre - pim * mim, pre * mim + pim * mre
        bre, bim = bre * bre - bim * bim, 2.0 * bre * bim
    return pre, pim


def _discretise(a_re, a_im, log_step):
    step = jnp.exp(log_step)
    mag = jnp.exp(a_re * step)
    lre = mag * jnp.cos(a_im * step)
    lim = mag * jnp.sin(a_im * step)
    den = a_re * a_re + a_im * a_im
    nr = lre - 1.0
    fre = (nr * a_re + lim * a_im) / den
    fim = (lim * a_re - nr * a_im) / den
    return lre, lim, fre, fim


def _ssm_weights_body(arow_re, arow_im, acol_re, acol_im, lsrow, lscol, bt_re, bt_im,
                      ct_re, ct_im, wt_ref, ws_ref, wo_ref, at_ref):
    t, cw = SSM_CHUNK, CHUNK_W
    lane_pair = lax.broadcasted_iota(jnp.int32, (1, PAIR_STATE), 1) // STATE
    row_pair = lax.broadcasted_iota(jnp.int32, (PAIR_STATE, 1), 0) // STATE

    ch = lax.broadcasted_iota(jnp.int32, (SSM_GROUP, LAG_W), 0)
    ln = lax.broadcasted_iota(jnp.int32, (SSM_GROUP, LAG_W), 1)
    tile_lag = jnp.where((ln % SSM_GROUP) == ch, 1.0, 0.0).astype(F32)

    step_rows = lax.broadcasted_iota(jnp.int32, (cw, 1), 0) // SSM_GROUP
    slot = lax.broadcasted_iota(jnp.int32, (1, LAG_W), 1) // SSM_GROUP
    step_lanes = lax.broadcasted_iota(jnp.int32, (1, cw), 1) // SSM_GROUP

    zt = [jnp.zeros((SSM_GROUP, LAG_W), F32) for _ in range(2)]
    for d in range(2):
        lre, lim, fre, fim = _discretise(arow_re[d, 0], arow_im[d, 0], lsrow[d, 0])
        bre, bim = bt_re[d, 0], bt_im[d, 0]
        bbre = fre * bre - fim * bim
        bbim = fre * bim + fim * bre
        expo = (t - 1 - step_rows) if d == 0 else step_rows
        pre, pim = _cpow(lre, lim, expo)
        tbre = jnp.tile(bbre, (t, 1))
        tbim = jnp.tile(bbim, (t, 1))
        inj_re = pre * tbre - pim * tbim
        inj_im = pre * tbim + pim * tbre
        for g2 in range(2):
            keep = lane_pair == g2
            rows = pl.ds(g2 * cw, cw)
            ws_ref[0, rows, pl.ds((2 * d) * PAIR_STATE, PAIR_STATE)] = (
                jnp.where(keep, inj_re, 0.0).astype(ws_ref.dtype))
            ws_ref[0, rows, pl.ds((2 * d + 1) * PAIR_STATE, PAIR_STATE)] = (
                jnp.where(keep, inj_im, 0.0).astype(ws_ref.dtype))
        dre, dim_ = _cpow(lre, lim, jnp.full((1, 1), t, jnp.int32))
        at_ref[0, pl.ds(2 * d, 1), :] = dre
        at_ref[0, pl.ds(2 * d + 1, 1), :] = dim_

        cre, cim, _, _ = _discretise(acol_re[d, 0], acol_im[d, 0], lscol[d, 0])
        ctl_re = jnp.dot(ct_re[d, 0], tile_lag, precision=HI, preferred_element_type=F32)
        ctl_im = jnp.dot(ct_im[d, 0], tile_lag, precision=HI, preferred_element_type=F32)
        lag = (slot - (t - 1)) if d == 0 else ((t - 1) - slot)
        valid = (lag >= 0) & (slot < 2 * t - 1)
        qre, qim = _cpow(cre, cim, jnp.maximum(lag, 0))
        r_re = jnp.where(valid, qre * ctl_re - qim * ctl_im, 0.0)
        r_im = jnp.where(valid, qre * ctl_im + qim * ctl_re, 0.0)
        for g2 in range(2):
            keep = lane_pair == g2
            zt[g2] = (zt[g2]
                      + jnp.dot(jnp.where(keep, bbre, 0.0), r_re, precision=HI,
                                preferred_element_type=F32)
                      - jnp.dot(jnp.where(keep, bbim, 0.0), r_im, precision=HI,
                                preferred_element_type=F32))

        out_pow = (step_lanes + 1) if d == 0 else (t - step_lanes)
        ore, oim = _cpow(cre, cim, out_pow)
        c_re = ctl_re[:, :cw]
        c_im = ctl_im[:, :cw]
        rd_re = ore * c_re - oim * c_im
        rd_im = -(ore * c_im + oim * c_re)
        for g2 in range(2):
            keep = row_pair == g2
            cols = pl.ds(g2 * cw, cw)
            wo_ref[0, pl.ds((2 * d) * PAIR_STATE, PAIR_STATE), cols] = (
                jnp.where(keep, rd_re, 0.0).astype(wo_ref.dtype))
            wo_ref[0, pl.ds((2 * d + 1) * PAIR_STATE, PAIR_STATE), cols] = (
                jnp.where(keep, rd_im, 0.0).astype(wo_ref.dtype))

    for g2 in range(2):
        for s in range(t):
            off = (t - 1 - s) * SSM_GROUP
            shifted = zt[g2] if off == 0 else pltpu.roll(zt[g2], LAG_W - off, 1)
            wt_ref[0, g2, pl.ds(s * SSM_GROUP, SSM_GROUP), :] = (
                shifted[:, :cw].astype(wt_ref.dtype))


def _ssm_weights(a_re, a_im, log_step, b_re, b_im, c_re, c_im):
    np_, g2p = N_GROUP_PAIRS, PAIR_STATE
    arow = lambda a: a.reshape(2, np_, 1, g2p)
    acol = lambda a: a.reshape(2, np_, g2p, 1)
    ls = jnp.repeat(log_step, STATE, axis=-1)
    bt = lambda w: (w.reshape(2, np_, 2, STATE, SSM_GROUP)
                    .transpose(0, 1, 4, 2, 3).reshape(2, np_, SSM_GROUP, g2p))
    ct = lambda w: (w.reshape(2, np_, 2, SSM_GROUP, STATE)
                    .transpose(0, 1, 2, 4, 3).reshape(2, np_, g2p, SSM_GROUP))
    ins = [arow(a_re), arow(a_im), acol(a_re), acol(a_im), arow(ls), acol(ls),
           bt(b_re), bt(b_im), ct(c_re), ct(c_im)]
    spec = lambda x: pl.BlockSpec((2, 1) + x.shape[2:], lambda k: (0, k, 0, 0))
    return pl.pallas_call(
        _ssm_weights_body,
        grid=(np_,),
        in_specs=[spec(x) for x in ins],
        out_specs=[pl.BlockSpec((1, 2, CHUNK_W, CHUNK_W), lambda k: (k, 0, 0, 0)),
                   pl.BlockSpec((1, 2 * CHUNK_W, N_SLABS * g2p), lambda k: (k, 0, 0)),
                   pl.BlockSpec((1, N_SLABS * g2p, 2 * CHUNK_W), lambda k: (k, 0, 0)),
                   pl.BlockSpec((1, N_SLABS, g2p), lambda k: (k, 0, 0))],
        out_shape=[jax.ShapeDtypeStruct((np_, 2, CHUNK_W, CHUNK_W), BF16),
                   jax.ShapeDtypeStruct((np_, 2 * CHUNK_W, N_SLABS * g2p), BF16),
                   jax.ShapeDtypeStruct((np_, N_SLABS * g2p, 2 * CHUNK_W), BF16),
                   jax.ShapeDtypeStruct((np_, N_SLABS, g2p), F32)],
        compiler_params=_params(("parallel",)),
        name="ssm_weights",
    )(*ins)


def _ssm_inject_body(x_ref, ws_ref, s_ref):
    s = jnp.dot(x_ref[0], ws_ref[0], preferred_element_type=F32)
    for slab in range(N_SLABS):
        s_ref[slab] = s[:, slab * PAIR_STATE:(slab + 1) * PAIR_STATE]


def _ssm_inject(x, ws):
    np_, r, _ = x.shape
    return pl.pallas_call(
        _ssm_inject_body,
        grid=(np_,),
        in_specs=[pl.BlockSpec((1, r, 2 * CHUNK_W), lambda k: (k, 0, 0)),
                  pl.BlockSpec((1,) + ws.shape[1:], lambda k: (k, 0, 0))],
        out_specs=pl.BlockSpec((N_SLABS, r, PAIR_STATE), lambda k: (0, 0, k)),
        out_shape=jax.ShapeDtypeStruct((N_SLABS, r, np_ * PAIR_STATE), F32),
        compiler_params=_params(("parallel",)),
        name="ssm_inject",
    )(x, ws)


def _ssm_scan_body(s_ref, at_ref, hin_ref, *, tiles_per_seq, n_tiles):
    sub = V7X_SUBLANES
    lanes = s_ref.shape[-1]
    a = [at_ref[pl.ds(slab, 1), :] for slab in range(N_SLABS)]

    def sweep(d, tile, carry):
        edge = 0 if d == 0 else tiles_per_seq - 1
        keep = jnp.where(lax.rem(tile, tiles_per_seq) == edge,
                         jnp.float32(0.0), jnp.float32(1.0))
        hre, him = carry[0] * keep, carry[1] * keep
        are, aim = a[2 * d], a[2 * d + 1]
        row0 = pl.multiple_of(tile * V7X_SUBLANES, V7X_SUBLANES)
        sre = s_ref[2 * d, pl.ds(row0, V7X_SUBLANES), :]
        sim = s_ref[2 * d + 1, pl.ds(row0, V7X_SUBLANES), :]
        out_re = [None] * sub
        out_im = [None] * sub
        order = range(sub) if d == 0 else range(sub - 1, -1, -1)
        for q in order:
            out_re[q], out_im[q] = hre, him
            xr = sre[q:q + 1]
            xi = sim[q:q + 1]
            hre, him = are * hre - aim * him + xr, are * him + aim * hre + xi
        hin_ref[2 * d, pl.ds(row0, V7X_SUBLANES), :] = jnp.concatenate(out_re, axis=0)
        hin_ref[2 * d + 1, pl.ds(row0, V7X_SUBLANES), :] = jnp.concatenate(out_im, axis=0)
        return hre, him

    zero = (jnp.zeros((1, lanes), F32), jnp.zeros((1, lanes), F32))
    lax.fori_loop(0, n_tiles, lambda i, c: sweep(0, i, c), zero)
    lax.fori_loop(0, n_tiles, lambda i, c: sweep(1, n_tiles - 1 - i, c), zero)


def _ssm_scan(s, at, chunks_per_seq, lane_block):
    _, r, lanes = s.shape
    assert chunks_per_seq % V7X_SUBLANES == 0 and r % chunks_per_seq == 0
    return pl.pallas_call(
        functools.partial(_ssm_scan_body, tiles_per_seq=chunks_per_seq // V7X_SUBLANES,
                          n_tiles=r // V7X_SUBLANES),
        grid=(lanes // lane_block,),
        in_specs=[pl.BlockSpec((N_SLABS, r, lane_block), lambda k: (0, 0, k)),
                  pl.BlockSpec((N_SLABS, lane_block), lambda k: (0, k))],
        out_specs=pl.BlockSpec((N_SLABS, r, lane_block), lambda k: (0, 0, k)),
        out_shape=jax.ShapeDtypeStruct(s.shape, F32),
        compiler_params=_params(("parallel",)),
        name="ssm_scan",
    )(s, at)


def _ssm_mix_body(x_ref, wt_ref, hin_ref, wo_ref, y_ref):
    x = x_ref[0]
    hin = jnp.concatenate([hin_ref[slab] for slab in range(N_SLABS)], axis=1).astype(BF16)
    carried = jnp.dot(hin, wo_ref[0], preferred_element_type=F32)
    for g2 in range(2):
        cols = slice(g2 * CHUNK_W, (g2 + 1) * CHUNK_W)
        y_ref[0, :, cols] = (carried[:, cols] + jnp.dot(
            x[:, cols], wt_ref[0, g2], preferred_element_type=F32)).astype(y_ref.dtype)


def _ssm_mix(x, wt, hin, wo):
    np_, r, _ = x.shape
    return pl.pallas_call(
        _ssm_mix_body,
        grid=(np_,),
        in_specs=[pl.BlockSpec((1, r, 2 * CHUNK_W), lambda k: (k, 0, 0)),
                  pl.BlockSpec((1,) + wt.shape[1:], lambda k: (k, 0, 0, 0)),
                  pl.BlockSpec((N_SLABS, r, PAIR_STATE), lambda k: (0, 0, k)),
                  pl.BlockSpec((1,) + wo.shape[1:], lambda k: (k, 0, 0))],
        out_specs=pl.BlockSpec((1, r, 2 * CHUNK_W), lambda k: (k, 0, 0)),
        out_shape=jax.ShapeDtypeStruct((np_, r, 2 * CHUNK_W), BF16),
        compiler_params=_params(("parallel",)),
        name="ssm_mix",
    )(x, wt, hin, wo)


def _outproj_body(x_ref, a_ref, yc_ref, u_ref, d_ref, wg_ref, bg_ref, gs_ref, wo_ref, gf_ref,
                  laneperm_t_ref, rowperm_t_ref, x1_ref, h_ref):
    y_ssm = _from_chunk_layout(yc_ref, laneperm_t_ref, rowperm_t_ref)
    y = jax.nn.gelu(y_ssm + d_ref[...] * u_ref[...])
    gate = jnp.dot(y.astype(BF16), wg_ref[...], preferred_element_type=F32) + bg_ref[...]
    s = _rms(y * jax.nn.sigmoid(gate), gs_ref[...], NORM_EPS)
    mixed = (jnp.dot(a_ref[...], wo_ref[:ATTN_WIDTH, :], preferred_element_type=F32)
             + jnp.dot(s.astype(BF16), wo_ref[ATTN_WIDTH:, :], preferred_element_type=F32))
    x1 = x_ref[...] + mixed
    x1_ref[...] = x1
    h_ref[...] = _rms(x1, gf_ref[...], NORM_EPS).astype(h_ref.dtype)


def _outproj(x2, a2, yc, u2, d, w_glu, b_glu, g_ssm, w_out, g_ffn, laneperm_t, rowperm_t):
    n_tok = x2.shape[0]
    tm = RELAYOUT_ROWS
    row = lambda w: pl.BlockSpec((tm, w), lambda i: (i, 0))
    full = lambda x: pl.BlockSpec(x.shape, lambda i: (0, 0))
    chunk_block = pl.BlockSpec((N_GROUP_PAIRS, CHUNKS_PER_TILE, 2 * CHUNK_W), lambda i: (0, i, 0))
    return pl.pallas_call(
        _outproj_body,
        grid=(n_tok // tm,),
        in_specs=[row(D_MODEL), row(ATTN_WIDTH), chunk_block, row(SSM_WIDTH),
                  full(d), full(w_glu), full(b_glu), full(g_ssm), full(w_out), full(g_ffn),
                  full(laneperm_t), full(rowperm_t)],
        out_specs=[row(D_MODEL), row(D_MODEL)],
        out_shape=[jax.ShapeDtypeStruct((n_tok, D_MODEL), F32),
                   jax.ShapeDtypeStruct((n_tok, D_MODEL), BF16)],
        compiler_params=_params(("parallel",)),
        name="outproj",
    )(x2, a2, yc, u2, d, w_glu, b_glu, g_ssm, w_out, g_ffn, laneperm_t, rowperm_t)


def _ffn_body(h_ref, hp_ref, hn_ref, x1_ref, wu_ref, cw_ref, cb_ref, wd_ref, gf_ref, o_ref,
              hcat_ref, *, tm, tiles_per_seq, ff_chunk):
    i = pl.program_id(0)
    halo = V7X_BF16_ROWS
    first = (i % tiles_per_seq) == 0
    last = (i % tiles_per_seq) == tiles_per_seq - 1
    zeros = jnp.zeros((halo, D_MODEL), hcat_ref.dtype)
    hcat_ref[:halo, :] = jnp.where(first, zeros, hp_ref[...])
    hcat_ref[halo:halo + tm, :] = h_ref[...]
    hcat_ref[halo + tm:, :] = jnp.where(last, zeros, hn_ref[...])
    hcat = hcat_ref[...]
    rows = tm + 2 * halo

    def conv(z, col0):
        cols = pl.ds(col0, ff_chunk)
        prev = pltpu.roll(z, 1, 0)[halo:halo + tm]
        nxt = pltpu.roll(z, rows - 1, 0)[halo:halo + tm]
        return (prev * cw_ref[pl.ds(0, 1), cols] + z[halo:halo + tm] * cw_ref[pl.ds(1, 1), cols]
                + nxt * cw_ref[pl.ds(2, 1), cols] + cb_ref[:, cols])

    def up(c):
        g0 = c * ff_chunk
        return (jnp.dot(hcat, wu_ref[:, pl.ds(g0, ff_chunk)], preferred_element_type=F32),
                jnp.dot(hcat, wu_ref[:, pl.ds(D_FF + g0, ff_chunk)],
                        preferred_element_type=F32))

    n_chunks = D_FF // ff_chunk
    acc = jnp.zeros((tm, D_MODEL), F32)
    z_next = up(0)
    for c in range(n_chunks):
        zg, zv = z_next
        if c + 1 < n_chunks:
            z_next = up(c + 1)
        g0 = c * ff_chunk
        act = (jax.nn.gelu(conv(zg, g0)) * conv(zv, D_FF + g0)).astype(BF16)
        acc = acc + jnp.dot(act, wd_ref[pl.ds(g0, ff_chunk), :], preferred_element_type=F32)
    o_ref[...] = _rms(x1_ref[...] + acc, gf_ref[...], NORM_EPS)


def _ffn(h2, x1, w_up, conv_w, conv_b, w_down, g_final, tm, seq, ff_chunk):
    n_tok = h2.shape[0]
    halo = V7X_BF16_ROWS
    per = tm // halo
    n_halo = n_tok // halo
    row = lambda w: pl.BlockSpec((tm, w), lambda i: (i, 0))
    once = lambda x: pl.BlockSpec(x.shape, lambda i: (0, 0), pipeline_mode=pl.Buffered(1))
    return pl.pallas_call(
        functools.partial(_ffn_body, tm=tm, tiles_per_seq=seq // tm, ff_chunk=ff_chunk),
        grid=(n_tok // tm,),
        in_specs=[row(D_MODEL),
                  pl.BlockSpec((halo, D_MODEL), lambda i: (jnp.maximum(i * per - 1, 0), 0)),
                  pl.BlockSpec((halo, D_MODEL),
                               lambda i: (jnp.minimum((i + 1) * per, n_halo - 1), 0)),
                  row(D_MODEL), once(w_up), once(conv_w), once(conv_b), once(w_down),
                  once(g_final)],
        out_specs=row(D_MODEL),
        out_shape=jax.ShapeDtypeStruct((n_tok, D_MODEL), F32),
        scratch_shapes=[pltpu.VMEM((tm + 2 * halo, D_MODEL), BF16)],
        compiler_params=_params(("parallel",)),
        name="ffn",
    )(h2, h2, h2, x1, w_up, conv_w, conv_b, w_down, g_final)


def _tile(pref, n):
    t = min(pref, n)
    assert n % t == 0, (pref, n)
    return t


def _trunk(x, p, ssm_w, perms):
    batch, seq, _ = x.shape
    n_tok = batch * seq
    x2 = x.reshape(n_tok, D_MODEL)
    assert seq % RELAYOUT_ROWS == 0
    rowperm, laneperm, laneperm_t, rowperm_t = perms

    q, k, vt, u, xc = _inproj(x2, p["g_mix"], p["w_in"], rowperm, laneperm)

    shape3 = (batch, seq, ATTN_WIDTH)
    a = _attention(q.reshape(shape3), k.reshape(shape3), vt,
                   p["lq1"], p["lk1"], p["lq2"], p["lk2"], p["g_subln"],
                   _tile(ATTN_Q_TILE, seq), _tile(ATTN_K_TILE, seq))

    wt, ws, wo, at = ssm_w
    s_in = _ssm_inject(xc, ws)
    at_flat = at.transpose(1, 0, 2).reshape(N_SLABS, N_GROUP_PAIRS * PAIR_STATE)
    hin = _ssm_scan(s_in, at_flat, seq // SSM_CHUNK, 4 * PAIR_STATE)
    yc = _ssm_mix(xc, wt, hin, wo)

    x1, h = _outproj(x2, a.reshape(n_tok, ATTN_WIDTH), yc, u, p["d"], p["w_glu"], p["b_glu"],
                     p["g_ssm"], p["w_out"], p["g_ffn"], laneperm_t, rowperm_t)
    out = _ffn(h, x1, p["w_up"], p["conv_w"], p["conv_b"], p["w_down"], p["g_final"],
               _tile(FFN_ROW_TILE, seq), seq, FFN_COL_CHUNK)
    return out.reshape(batch, seq, D_MODEL)


def kernel(x_prompt, x_sample, g_mix_norm, w_in, lambda_q1, lambda_k1, lambda_q2, lambda_k2, g_subln, ssm_a_re, ssm_a_im, ssm_log_step, ssm_b_re, ssm_b_im, ssm_c_re, ssm_c_im, ssm_d, w_glu, b_glu, g_ssm_out, w_out, g_ffn_norm, w_up, conv_w, conv_b, w_down, g_final):
    layer = 0
    vec = lambda t: t.reshape(1, -1).astype(F32)
    p = {
        "g_mix": vec(g_mix_norm[layer]),
        "w_in": w_in[layer].astype(BF16),
        "lq1": vec(lambda_q1[layer]), "lk1": vec(lambda_k1[layer]),
        "lq2": vec(lambda_q2[layer]), "lk2": vec(lambda_k2[layer]),
        "g_subln": g_subln[layer].reshape(-1, 1).astype(F32), "d": vec(ssm_d[layer]),
        "w_glu": w_glu[layer].astype(BF16), "b_glu": vec(b_glu[layer]),
        "g_ssm": vec(g_ssm_out[layer]), "w_out": w_out[layer].astype(BF16),
        "g_ffn": vec(g_ffn_norm[layer]), "w_up": w_up[layer].astype(BF16),
        "conv_w": conv_w[layer].astype(F32), "conv_b": vec(conv_b[layer]),
        "w_down": w_down[layer].astype(BF16), "g_final": vec(g_final),
    }
    f32 = lambda t: t[layer].astype(F32)
    ssm_w = _ssm_weights(f32(ssm_a_re), f32(ssm_a_im), f32(ssm_log_step), f32(ssm_b_re),
                         f32(ssm_b_im), f32(ssm_c_re), f32(ssm_c_im))
    perms = _relayout_constants()
    return _trunk(x_prompt, p, ssm_w, perms), _trunk(x_sample, p, ssm_w, perms)
```
